```python
import jax, jax.numpy as jnp
from jax import lax
import numpy as np

D_MODEL = 4096
BATCH = 2
SEQ = 4096
DEPTH = 2

POOL_WINDOWS = (2, 4, 8, 16)
N_POOL_GROUPS = 4
POOL_WIDTH = D_MODEL // 2
POOL_GROUP = POOL_WIDTH // N_POOL_GROUPS
DN_HEAD_DIM = 128
DN_WIDTH = D_MODEL // 2
DN_HEADS = DN_WIDTH // DN_HEAD_DIM
DN_CONV = 4
CHUNK = 64
IN_SIZES = (POOL_WIDTH, 3 * DN_WIDTH, DN_WIDTH, DN_HEADS, DN_HEADS, D_MODEL, D_MODEL)
IN_WIDTH = POOL_WIDTH + 4 * DN_WIDTH + 2 * DN_HEADS + 2 * D_MODEL
D_FF = ((8 * D_MODEL // 3 + 255) // 256) * 256
FFN_CONV = 3
N_MOD = 6
EPS = 1e-6

kernel_name = "hybrid_pool_deltanet_convffn_adaln"


def rms_norm(x, w):
    x32 = x.astype(jnp.float32)
    y = x32 * lax.rsqrt(jnp.mean(x32 * x32, axis=-1, keepdims=True) + EPS)
    return (y * w.astype(jnp.float32)).astype(x.dtype)


def l2_normalize(t):
    return t * lax.rsqrt(jnp.sum(t * t, axis=-1, keepdims=True) + EPS)


def causal_depthwise_conv(x, w, b=None):
    K = w.shape[0]
    S = x.shape[1]
    xp = jnp.pad(x, ((0, 0), (K - 1, 0), (0, 0)))
    y = xp[:, 0:S] * w[0]
    for j in range(1, K):
        y = y + xp[:, j:j + S] * w[j]
    if b is not None:
        y = y + b
    return y


def multiscale_pool(u):
    B, S, _ = u.shape
    ug = u.reshape(B, S, N_POOL_GROUPS, POOL_GROUP).astype(jnp.float32)
    cs = jnp.cumsum(ug, axis=1)
    t = jnp.arange(S)
    outs = []
    for gi, win in enumerate(POOL_WINDOWS):
        csg = cs[:, :, gi]
        lagged = jnp.pad(csg, ((0, 0), (win, 0), (0, 0)))[:, :S]
        count = jnp.minimum(t + 1, win).astype(jnp.float32)[None, :, None]
        outs.append((csg - lagged) / count)
    pooled = jnp.stack(outs, axis=2)
    return (pooled - ug).astype(u.dtype)


def gated_delta_rule(q, k, v, g, beta):
    B, S, H, Dk = q.shape
    Dv = v.shape[-1]
    C = CHUNK
    N = S // C
    q = l2_normalize(q) * (Dk ** -0.5)
    k = l2_normalize(k)

    def chunks(t):
        return t.reshape(B, N, C, H, -1).transpose(0, 3, 1, 2, 4)

    qc, kc, vc = chunks(q), chunks(k), chunks(v)
    gc = g.reshape(B, N, C, H).transpose(0, 3, 1, 2)
    bc = beta.reshape(B, N, C, H).transpose(0, 3, 1, 2)
    gcum = jnp.cumsum(gc, axis=-1)
    causal = jnp.tril(jnp.ones((C, C), dtype=bool))
    strict = jnp.tril(jnp.ones((C, C), dtype=bool), k=-1)
    decay = jnp.exp(jnp.where(causal, gcum[..., :, None] - gcum[..., None, :], -jnp.inf))
    kb = kc * bc[..., None]
    vb = vc * bc[..., None]
    L = jnp.where(strict, jnp.einsum('bhnck,bhnsk->bhncs', kb, kc) * decay, 0.0)
    eye = jnp.eye(C, dtype=jnp.float32)
    T = lax.linalg.triangular_solve(eye + L, jnp.broadcast_to(eye, L.shape),
                                    left_side=True, lower=True, unit_diagonal=True)
    u = jnp.einsum('bhncs,bhnsv->bhncv', T, vb)
    w = jnp.einsum('bhncs,bhnsk->bhnck', T, kb * jnp.exp(gcum)[..., None])
    qk = jnp.where(causal, jnp.einsum('bhnck,bhnsk->bhncs', qc, kc) * decay, 0.0)
    q_dec = qc * jnp.exp(gcum)[..., None]
    k_dec = kc * jnp.exp(gcum[..., -1:] - gcum)[..., None]
    g_last = jnp.exp(gcum[..., -1])

    def step(state, inp):
        qd, kd, qk_i, u_i, w_i, gl = inp
        v_new = u_i - jnp.einsum('bhck,bhkv->bhcv', w_i, state)
        o = (jnp.einsum('bhck,bhkv->bhcv', qd, state)
             + jnp.einsum('bhcs,bhsv->bhcv', qk_i, v_new))
        state = state * gl[..., None, None] + jnp.einsum('bhck,bhcv->bhkv', kd, v_new)
        return state, o

    xs = tuple(jnp.moveaxis(t, 2, 0) for t in (q_dec, k_dec, qk, u, w, g_last))
    init = jnp.zeros((B, H, Dk, Dv), jnp.float32)
    _, o = lax.scan(step, init, xs)
    return o.transpose(1, 0, 3, 2, 4).reshape(B, S, H, Dv)


def token_mixer(h, w_in, pool_w, pool_scale, pool_proj, dn_conv, dn_a_log, dn_dt_bias,
                dn_norm, dn_proj, w_o):
    B, S, _ = h.shape
    proj = h @ w_in
    idx = []
    acc = 0
    for sz in IN_SIZES[:-1]:
        acc += sz
        idx.append(acc)
    u_pool, qkv, z, b_logit, a_logit, gate_a, gate_b = jnp.split(proj, idx, axis=-1)

    pa = multiscale_pool(u_pool).reshape(B, S, N_POOL_GROUPS, POOL_GROUP)
    pa = jnp.einsum('bsgc,gcd->bsgd', pa, pool_w).reshape(B, S, POOL_WIDTH) * pool_scale
    y_a = pa @ pool_proj

    qkv = jax.nn.silu(causal_depthwise_conv(qkv, dn_conv))
    q, k, v = jnp.split(qkv, 3, axis=-1)
    shp = (B, S, DN_HEADS, DN_HEAD_DIM)
    q = q.reshape(shp).astype(jnp.float32)
    k = k.reshape(shp).astype(jnp.float32)
    v = v.reshape(shp).astype(jnp.float32)
    beta = jax.nn.sigmoid(b_logit.astype(jnp.float32))
    g = -jnp.exp(dn_a_log.astype(jnp.float32)) * jax.nn.softplus(
        a_logit.astype(jnp.float32) + dn_dt_bias.astype(jnp.float32))
    o = gated_delta_rule(q, k, v, g, beta)
    o = o * lax.rsqrt(jnp.mean(o * o, axis=-1, keepdims=True) + EPS) * dn_norm.astype(jnp.float32)
    o = o * jax.nn.silu(z.reshape(shp).astype(jnp.float32))
    y_b = o.reshape(B, S, DN_WIDTH).astype(h.dtype) @ dn_proj

    y = jax.nn.sigmoid(gate_a) * y_a + jax.nn.sigmoid(gate_b) * y_b
    return y @ w_o


def conv_ffn(h, w_up, ffn_conv, ffn_conv_b, w_down):
    up = h @ w_up
    a, u = jnp.split(up, 2, axis=-1)
    a = causal_depthwise_conv(a, ffn_conv, ffn_conv_b)
    return (jax.nn.silu(a) * u) @ w_down


def setup_inputs(seed: int = 0) -> dict:
    key = jax.random.key(seed)
    ks = jax.random.split(key, 26)
    f32 = jnp.float32

    def nrm(k, shape, fan_in):
        return jax.random.normal(k, shape, f32) * (fan_in ** -0.5)

    def gain(k, shape):
        return 1.0 + 0.05 * jax.random.normal(k, shape, f32)

    dt = jnp.exp(jax.random.uniform(ks[12], (DEPTH, DN_HEADS), f32,
                                    minval=math_log(0.001), maxval=math_log(0.1)))
    return {
        "x": jax.random.normal(ks[0], (BATCH, SEQ, D_MODEL), f32),
        "c": jax.random.normal(ks[1], (BATCH, D_MODEL), f32),
        "w_ada": nrm(ks[2], (D_MODEL, N_MOD * D_MODEL), D_MODEL),
        "b_ada": 0.02 * jax.random.normal(ks[3], (N_MOD * D_MODEL,), f32),
        "ada_table": 0.1 * jax.random.normal(ks[4], (DEPTH, N_MOD, D_MODEL), f32),
        "mix_pre_norm": gain(ks[5], (DEPTH, D_MODEL)),
        "w_in": nrm(ks[6], (DEPTH, D_MODEL, IN_WIDTH), D_MODEL),
        "pool_w": nrm(ks[7], (DEPTH, N_POOL_GROUPS, POOL_GROUP, POOL_GROUP), POOL_GROUP),
        "pool_scale": 1.0 + 0.1 * jax.random.normal(ks[8], (DEPTH, POOL_WIDTH), f32),
        "pool_proj": nrm(ks[9], (DEPTH, POOL_WIDTH, D_MODEL), POOL_WIDTH),
        "dn_conv": nrm(ks[10], (DEPTH, DN_CONV, 3 * DN_WIDTH), DN_CONV),
        "dn_a_log": jnp.log(jax.random.uniform(ks[11], (DEPTH, DN_HEADS), f32, minval=1.0, maxval=16.0)),
        "dn_dt_bias": dt + jnp.log(-jnp.expm1(-dt)),
        "dn_norm": gain(ks[13], (DEPTH, DN_HEAD_DIM)),
        "dn_proj": nrm(ks[14], (DEPTH, DN_WIDTH, D_MODEL), DN_WIDTH),
        "w_o": nrm(ks[15], (DEPTH, D_MODEL, D_MODEL), D_MODEL),
        "mix_post_norm": gain(ks[16], (DEPTH, D_MODEL)),
        "ffn_pre_norm": gain(ks[17], (DEPTH, D_MODEL)),
        "w_up": nrm(ks[18], (DEPTH, D_MODEL, 2 * D_FF), D_MODEL),
        "ffn_conv": nrm(ks[19], (DEPTH, FFN_CONV, D_FF), FFN_CONV),
        "ffn_conv_b": 0.02 * jax.random.normal(ks[20], (DEPTH, D_FF), f32),
        "w_down": nrm(ks[21], (DEPTH, D_FF, D_MODEL), D_FF),
        "ffn_post_norm": gain(ks[22], (DEPTH, D_MODEL)),
    }


def math_log(v):
    return float(np.log(v))


def reference(x, c, w_ada, b_ada, ada_table, mix_pre_norm, w_in, pool_w, pool_scale, pool_proj,
              dn_conv, dn_a_log, dn_dt_bias, dn_norm, dn_proj, w_o, mix_post_norm,
              ffn_pre_norm, w_up, ffn_conv, ffn_conv_b, w_down, ffn_post_norm):
    B = x.shape[0]
    mod0 = (jax.nn.silu(c) @ w_ada + b_ada).reshape(B, N_MOD, D_MODEL)
    for l in range(DEPTH):
        mod = mod0 + ada_table[l]
        shift_m, scale_m, gate_m = mod[:, 0, None], mod[:, 1, None], mod[:, 2, None]
        shift_f, scale_f, gate_f = mod[:, 3, None], mod[:, 4, None], mod[:, 5, None]

        h = rms_norm(x, mix_pre_norm[l]) * (1.0 + scale_m) + shift_m
        y = token_mixer(h, w_in[l], pool_w[l], pool_scale[l], pool_proj[l], dn_conv[l],
                        dn_a_log[l], dn_dt_bias[l], dn_norm[l], dn_proj[l], w_o[l])
        x = x + gate_m * rms_norm(y, mix_post_norm[l])

        h = rms_norm(x, ffn_pre_norm[l]) * (1.0 + scale_f) + shift_f
        y = conv_ffn(h, w_up[l], ffn_conv[l], ffn_conv_b[l], w_down[l])
        x = x + gate_f * rms_norm(y, ffn_post_norm[l])
    return x
```

```python
import functools

import jax
import jax.numpy as jnp
from jax import lax
from jax.experimental import pallas as pl
from jax.experimental.pallas import tpu as pltpu

F32 = jnp.float32
BF16 = jnp.bfloat16

EPS = 1e-6
POOL_WINDOWS = (2, 4, 8, 16)
HEAD_DIM = 128
DN_CONV_TAPS = 4
FFN_CONV_TAPS = 3
N_MOD = 6
CHUNK = 128
HALO = 16
VMEM_LIMIT_BYTES = 56 * 1024 * 1024
LANES = 128


def _params(*semantics):
    return pltpu.CompilerParams(dimension_semantics=semantics,
                                vmem_limit_bytes=VMEM_LIMIT_BYTES)


def _dot(a, b):
    return jnp.dot(a, b, preferred_element_type=F32)


def _dot_nt(a, b):
    return lax.dot_general(a, b, (((1,), (1,)), ((), ())), preferred_element_type=F32)


def _sigmoid(x):
    return 1.0 / (1.0 + jnp.exp(-x))


def _silu(x):
    return x * _sigmoid(x)


def _softplus(x):
    return jnp.maximum(x, 0.0) + jnp.log1p(jnp.exp(-jnp.abs(x)))


def _rms(x):
    return x * lax.rsqrt(jnp.mean(x * x, axis=-1, keepdims=True) + EPS)


def _ada_kernel(c_ref, w_ref, b_ref, t_ref, o_ref):
    a = _silu(c_ref[...]).astype(BF16)
    acc = _dot(a, w_ref[...].astype(BF16)) + b_ref[...]
    o_ref[...] = acc[None] + t_ref[...]


def _ada(c8, w_ada, b_ada, table):
    depth = table.shape[0]
    d, n = w_ada.shape
    tn = min(512, n)
    return pl.pallas_call(
        _ada_kernel,
        grid=(n // tn,),
        in_specs=[pl.BlockSpec((8, d), lambda j: (0, 0)),
                  pl.BlockSpec((d, tn), lambda j: (0, j)),
                  pl.BlockSpec((1, tn), lambda j: (0, j)),
                  pl.BlockSpec((depth, 1, tn), lambda j: (0, 0, j))],
        out_specs=pl.BlockSpec((depth, 8, tn), lambda j: (0, 0, j)),
        out_shape=jax.ShapeDtypeStruct((depth, 8, n), F32),
        compiler_params=_params("arbitrary"),
        name="ada",
    )(c8, w_ada, b_ada.reshape(1, n), table.reshape(depth, 1, n))


def _norm_mod_kernel(x_ref, w_ref, sc_ref, sh_ref, h_ref):
    y = _rms(x_ref[...]) * w_ref[...]
    h_ref[...] = (y * (1.0 + sc_ref[0]) + sh_ref[0]).astype(h_ref.dtype)


def _norm_mod(x, w, scale, shift, seq):
    m, d = x.shape
    tm = min(256, seq)
    per_seq = seq // tm
    row = pl.BlockSpec((tm, d), lambda i: (i, 0))
    vec = pl.BlockSpec((1, d), lambda i: (0, 0))
    bvec = pl.BlockSpec((1, 1, d), lambda i: (i // per_seq, 0, 0))
    return pl.pallas_call(
        _norm_mod_kernel,
        grid=(m // tm,),
        in_specs=[row, vec, bvec, bvec],
        out_specs=row,
        out_shape=jax.ShapeDtypeStruct((m, d), BF16),
        compiler_params=_params("parallel"),
        name="norm_mod",
    )(x, w.reshape(1, d), scale, shift)


def _resid_norm_kernel(y_ref, x_ref, pw_ref, g_ref, nw_ref, sc_ref, sh_ref, xo_ref, h_ref):
    xn = x_ref[...] + g_ref[0] * (_rms(y_ref[...].astype(F32)) * pw_ref[...])
    xo_ref[...] = xn
    h = _rms(xn) * nw_ref[...]
    h_ref[...] = (h * (1.0 + sc_ref[0]) + sh_ref[0]).astype(h_ref.dtype)


def _resid_kernel(y_ref, x_ref, pw_ref, g_ref, xo_ref):
    xo_ref[...] = x_ref[...] + g_ref[0] * (_rms(y_ref[...].astype(F32)) * pw_ref[...])


def _resid_norm(y, x, post_w, gate, seq, nxt=None):
    m, d = x.shape
    tm = min(256, seq)
    per_seq = seq // tm
    row = pl.BlockSpec((tm, d), lambda i: (i, 0))
    vec = pl.BlockSpec((1, d), lambda i: (0, 0))
    bvec = pl.BlockSpec((1, 1, d), lambda i: (i // per_seq, 0, 0))
    if nxt is None:
        return pl.pallas_call(
            _resid_kernel,
            grid=(m // tm,),
            in_specs=[row, row, vec, bvec],
            out_specs=row,
            out_shape=jax.ShapeDtypeStruct((m, d), F32),
            compiler_params=_params("parallel"),
            name="resid",
        )(y, x, post_w.reshape(1, d), gate)
    norm_w, scale, shift = nxt
    return pl.pallas_call(
        _resid_norm_kernel,
        grid=(m // tm,),
        in_specs=[row, row, vec, bvec, vec, bvec, bvec],
        out_specs=[row, row],
        out_shape=[jax.ShapeDtypeStruct((m, d), F32), jax.ShapeDtypeStruct((m, d), BF16)],
        compiler_params=_params("parallel"),
        name="resid_norm",
    )(y, x, post_w.reshape(1, d), gate, norm_w.reshape(1, d), scale, shift)


def _mm_kernel(a_ref, w_ref, o_ref):
    o_ref[...] = _dot(a_ref[...], w_ref[...]).astype(o_ref.dtype)


def _matmul(a, w, out_dtype, tm, tn, name):
    m, k = a.shape
    n = w.shape[1]
    tm, tn = min(tm, m), min(tn, n)
    return pl.pallas_call(
        _mm_kernel,
        grid=(n // tn, m // tm),
        in_specs=[pl.BlockSpec((tm, k), lambda j, i: (i, 0)),
                  pl.BlockSpec((k, tn), lambda j, i: (0, j))],
        out_specs=pl.BlockSpec((tm, tn), lambda j, i: (i, j)),
        out_shape=jax.ShapeDtypeStruct((m, n), out_dtype),
        compiler_params=_params("parallel", "arbitrary"),
        name=name,
    )(a, w)


def _ba_kernel(heads, h_ref, w_ref, alog_ref, dtb_ref, o_ref):
    p = _dot(h_ref[...], w_ref[...])
    tm = p.shape[0]
    lane = lax.broadcasted_iota(jnp.int32, p.shape, 1)
    is_g = (lane >= heads) & (lane < 2 * heads)
    g = jnp.where(is_g, -jnp.exp(alog_ref[...]) * _softplus(p + dtb_ref[...]), 0.0)
    pos = lax.broadcasted_iota(jnp.int32, p.shape, 0) % CHUNK
    shift = 1
    while shift < CHUNK:
        g = g + jnp.where(pos >= shift, pltpu.roll(g, shift, axis=0), 0.0)
        shift *= 2
    del tm
    o_ref[...] = jnp.where(is_g, g, _sigmoid(p))


def _ba(h, w_ba, a_log, dt_bias, heads):
    m, k = h.shape
    tm = min(512, m)
    pad = lambda v: jnp.zeros((1, LANES), F32).at[0, heads:2 * heads].set(v)
    return pl.pallas_call(
        functools.partial(_ba_kernel, heads),
        grid=(m // tm,),
        in_specs=[pl.BlockSpec((tm, k), lambda i: (i, 0)),
                  pl.BlockSpec((k, LANES), lambda i: (0, 0)),
                  pl.BlockSpec((1, LANES), lambda i: (0, 0)),
                  pl.BlockSpec((1, LANES), lambda i: (0, 0))],
        out_specs=pl.BlockSpec((tm, LANES), lambda i: (i, 0)),
        out_shape=jax.ShapeDtypeStruct((m, LANES), F32),
        compiler_params=_params("parallel"),
        name="dn_gates",
    )(h, w_ba, pad(a_log), pad(dt_bias))


def _pool_kernel(per_seq, u_ref, halo_ref, pw_ref, sc_ref, o_ref, buf_ref, pa_ref):
    tm, width = u_ref.shape
    group = width // len(POOL_WINDOWS)
    t = pl.program_id(0) % per_seq
    halo = halo_ref[...].astype(F32)
    buf_ref[0:HALO, :] = jnp.where(t == 0, 0.0, halo)
    buf_ref[HALO:, :] = u_ref[...].astype(F32)
    pos = t * tm + lax.broadcasted_iota(jnp.int32, (tm, LANES), 0)
    for gi, win in enumerate(POOL_WINDOWS):
        inv_count = 1.0 / jnp.minimum(pos + 1, win).astype(F32)
        for c in range(gi * group, (gi + 1) * group, LANES):
            tok = buf_ref[HALO:HALO + tm, c:c + LANES]
            acc = tok
            for j in range(1, win):
                acc = acc + buf_ref[HALO - j:HALO - j + tm, c:c + LANES]
            pa_ref[:, c:c + LANES] = (acc * inv_count - tok).astype(BF16)
    for gi in range(len(POOL_WINDOWS)):
        cols = slice(gi * group, (gi + 1) * group)
        r = _dot(pa_ref[:, cols], pw_ref[gi]) * sc_ref[:, cols]
        o_ref[:, cols] = r.astype(o_ref.dtype)


def _pool(proj, pool_w, pool_scale, seq):
    m = proj.shape[0]
    groups, group, _ = pool_w.shape
    width = groups * group
    tm = min(256, seq)
    per_seq = seq // tm
    return pl.pallas_call(
        functools.partial(_pool_kernel, per_seq),
        grid=(m // tm,),
        in_specs=[pl.BlockSpec((tm, width), lambda i: (i, 0)),
                  pl.BlockSpec((HALO, width), lambda i: (jnp.maximum(i * (tm // HALO) - 1, 0), 0)),
                  pl.BlockSpec((groups, group, group), lambda i: (0, 0, 0)),
                  pl.BlockSpec((1, width), lambda i: (0, 0))],
        out_specs=pl.BlockSpec((tm, width), lambda i: (i, 0)),
        out_shape=jax.ShapeDtypeStruct((m, width), BF16),
        scratch_shapes=[pltpu.VMEM((tm + HALO, width), F32), pltpu.VMEM((tm, width), BF16)],
        compiler_params=_params("parallel"),
        name="pool_mixer",
    )(proj, proj, pool_w, pool_scale.reshape(1, width))


def _qkv_kernel(per_seq, q_ref, k_ref, v_ref, hq_ref, hk_ref, hv_ref, wq_ref, wk_ref, wv_ref,
                qo_ref, ko_ref, vo_ref, buf_ref):
    tm, width = q_ref.shape
    t = pl.program_id(0) % per_seq
    first = t == 0
    for x_ref, halo_ref, w_ref, o_ref, norm in (
            (q_ref, hq_ref, wq_ref, qo_ref, HEAD_DIM ** -0.5),
            (k_ref, hk_ref, wk_ref, ko_ref, 1.0),
            (v_ref, hv_ref, wv_ref, vo_ref, None)):
        buf_ref[0:HALO, :] = jnp.where(first, 0.0, halo_ref[...].astype(F32))
        buf_ref[HALO:, :] = x_ref[...].astype(F32)
        for c in range(0, width, HEAD_DIM):
            cols = slice(c, c + HEAD_DIM)
            y = None
            for j in range(DN_CONV_TAPS):
                r0 = HALO - (DN_CONV_TAPS - 1) + j
                term = buf_ref[r0:r0 + tm, cols] * w_ref[j:j + 1, cols]
                y = term if y is None else y + term
            y = _silu(y)
            if norm is not None:
                y = y * lax.rsqrt(jnp.sum(y * y, axis=-1, keepdims=True) + EPS) * norm
            o_ref[:, cols] = y.astype(o_ref.dtype)


def _qkv_conv(proj, dn_conv, width, col0, seq):
    m = proj.shape[0]
    tm = min(128, seq)
    per_seq = seq // tm
    b0 = col0 // width
    rows = [pl.BlockSpec((tm, width), functools.partial(lambda s, i: (i, b0 + s), s)) for s in range(3)]
    halos = [pl.BlockSpec((HALO, width),
                          functools.partial(lambda s, i: (jnp.maximum(i * (tm // HALO) - 1, 0), b0 + s), s))
             for s in range(3)]
    taps = [pl.BlockSpec((DN_CONV_TAPS, width), functools.partial(lambda s, i: (0, s), s)) for s in range(3)]
    out = pl.BlockSpec((tm, width), lambda i: (i, 0))
    shp = jax.ShapeDtypeStruct((m, width), BF16)
    return pl.pallas_call(
        functools.partial(_qkv_kernel, per_seq),
        grid=(m // tm,),
        in_specs=rows + halos + taps,
        out_specs=[out, out, out],
        out_shape=[shp, shp, shp],
        scratch_shapes=[pltpu.VMEM((tm + HALO, width), F32)],
        compiler_params=_params("parallel"),
        name="dn_qkv_conv",
    )(proj, proj, proj, proj, proj, proj, dn_conv, dn_conv, dn_conv)


def _split(a):
    hi = a.astype(BF16)
    return hi, (a - hi.astype(F32)).astype(BF16)


def _dot_split(a, b):
    ah, al = _split(a)
    bh, bl = _split(b)
    return _dot(ah, bh) + _dot(ah, bl) + _dot(al, bh)


def _unit_lower_inverse(low):
    n = low.shape[0]
    ri = lax.broadcasted_iota(jnp.int32, (n, n), 0)
    ci = lax.broadcasted_iota(jnp.int32, (n, n), 1)
    same = lambda bits: jnp.right_shift(ri, bits) == jnp.right_shift(ci, bits)
    inv = jnp.where(ri == ci, 1.0, 0.0) - jnp.where(same(1), low, 0.0)
    bits = 1
    while (1 << bits) < n:
        off = jnp.where(same(bits + 1) & jnp.logical_not(same(bits)), low, 0.0)
        inv = inv - _dot_split(inv, _dot_split(off, inv))
        bits += 1
    return inv


def _delta_kernel(heads, q_ref, k_ref, v_ref, z_ref, g_ref, gt_ref, nw_ref, o_ref, state_ref):
    rows, width = q_ref.shape
    hp = width // HEAD_DIM
    c = CHUNK

    @pl.when(pl.program_id(2) == 0)
    def _():
        state_ref[...] = jnp.zeros_like(state_ref)

    gates = g_ref[...]
    lane = lax.broadcasted_iota(jnp.int32, gates.shape, 1)
    ri = lax.broadcasted_iota(jnp.int32, (c, c), 0)
    ci = lax.broadcasted_iota(jnp.int32, (c, c), 1)
    causal = ri >= ci
    strict = ri > ci
    for hh in range(hp):
        head = pl.program_id(1) * hp + hh
        cols = slice(hh * HEAD_DIM, (hh + 1) * HEAD_DIM)
        beta_col = jnp.sum(jnp.where(lane == head, gates, 0.0), axis=-1, keepdims=True)
        gc_col = jnp.sum(jnp.where(lane == heads + head, gates, 0.0), axis=-1, keepdims=True)
        gc_row = gt_ref[pl.ds(heads + head, 1), :]
        state = state_ref[hh]
        for r0 in range(0, rows, c):
            rsl = slice(r0, r0 + c)
            q = q_ref[rsl, cols].astype(F32)
            k = k_ref[rsl, cols].astype(F32)
            v = v_ref[rsl, cols].astype(F32)
            beta = beta_col[rsl]
            gcc = gc_col[rsl]
            gcr = gc_row[:, rsl]
            g_last = gcr[:, c - 1:c]
            decay = jnp.exp(jnp.where(causal, gcc - gcr, -jnp.inf))
            e_g = jnp.exp(gcc)
            kb = k * beta
            k16 = k.astype(BF16)
            low = jnp.where(strict, _dot_nt(kb.astype(BF16), k16) * decay, 0.0)
            qk = jnp.where(causal, _dot_nt(q.astype(BF16), k16) * decay, 0.0)
            t_inv = _unit_lower_inverse(low)
            t16 = t_inv.astype(BF16)
            u = _dot(t16, (v * beta).astype(BF16))
            w = _dot(t16, (kb * e_g).astype(BF16))
            s16 = state.astype(BF16)
            v_new = u - _dot(w.astype(BF16), s16)
            vn16 = v_new.astype(BF16)
            o = _dot((q * e_g).astype(BF16), s16) + _dot(qk.astype(BF16), vn16)
            k_dec = k * jnp.exp(g_last - gcc)
            state = state * jnp.exp(g_last) + _dot(k_dec.T.astype(BF16), vn16)
            z = z_ref[rsl, cols].astype(F32)
            o = _rms(o) * nw_ref[...] * _silu(z)
            o_ref[rsl, cols] = o.astype(o_ref.dtype)
        state_ref[hh] = state


def _delta(q, k, v, proj, z_col0, gates, gates_t, dn_norm, batch, seq, heads):
    m, width = q.shape
    hp = 2
    rows = min(2 * CHUNK, seq)
    per_seq = seq // rows
    bw = hp * HEAD_DIM
    zb0 = z_col0 // bw
    qkv = pl.BlockSpec((rows, bw), lambda b, h, t: (b * per_seq + t, h))
    return pl.pallas_call(
        functools.partial(_delta_kernel, heads),
        grid=(batch, heads // hp, per_seq),
        in_specs=[qkv, qkv, qkv,
                  pl.BlockSpec((rows, bw), lambda b, h, t: (b * per_seq + t, zb0 + h)),
                  pl.BlockSpec((rows, LANES), lambda b, h, t: (b * per_seq + t, 0)),
                  pl.BlockSpec((2 * heads, rows), lambda b, h, t: (0, b * per_seq + t)),
                  pl.BlockSpec((1, HEAD_DIM), lambda b, h, t: (0, 0))],
        out_specs=qkv,
        out_shape=jax.ShapeDtypeStruct((m, width), BF16),
        scratch_shapes=[pltpu.VMEM((hp, HEAD_DIM, HEAD_DIM), F32)],
        compiler_params=_params("parallel", "parallel", "arbitrary"),
        name="gated_delta",
    )(q, k, v, proj, gates, gates_t, dn_norm.reshape(1, HEAD_DIM))


def _merge_kernel(h_ref, pa_ref, ob_ref, wga_ref, wgb_ref, pp_ref, dp_ref, y_ref):
    h = h_ref[...]
    ya = _sigmoid(_dot(h, wga_ref[...])) * _dot(pa_ref[...], pp_ref[...])
    yb = _sigmoid(_dot(h, wgb_ref[...])) * _dot(ob_ref[...], dp_ref[...])
    y_ref[...] = (ya + yb).astype(y_ref.dtype)


def _merge(h, pa, ob, w_gates, pool_proj, dn_proj):
    m, d = h.shape
    kp = pa.shape[1]
    tm, tn = min(512, m), min(512, d)
    nb = d // tn
    return pl.pallas_call(
        _merge_kernel,
        grid=(nb, m // tm),
        in_specs=[pl.BlockSpec((tm, d), lambda j, i: (i, 0)),
                  pl.BlockSpec((tm, kp), lambda j, i: (i, 0)),
                  pl.BlockSpec((tm, kp), lambda j, i: (i, 0)),
                  pl.BlockSpec((d, tn), lambda j, i: (0, j)),
                  pl.BlockSpec((d, tn), lambda j, i: (0, nb + j)),
                  pl.BlockSpec((kp, tn), lambda j, i: (0, j)),
                  pl.BlockSpec((kp, tn), lambda j, i: (0, j))],
        out_specs=pl.BlockSpec((tm, tn), lambda j, i: (i, j)),
        out_shape=jax.ShapeDtypeStruct((m, d), BF16),
        compiler_params=_params("parallel", "arbitrary"),
        name="gated_merge",
    )(h, pa, ob, w_gates, w_gates, pool_proj, dn_proj)


FFN_CARRY = 8


def _ffn_up_kernel(per_seq, h_ref, wa_ref, wu_ref, cw_ref, cb_ref, o_ref, a_ref):
    tm = h_ref.shape[0]

    @pl.when(pl.program_id(1) % per_seq == 0)
    def _():
        a_ref[0:FFN_CARRY, :] = jnp.zeros((FFN_CARRY, a_ref.shape[1]), F32)

    h = h_ref[...]
    a_ref[FFN_CARRY:, :] = _dot(h, wa_ref[...])
    u = _dot(h, wu_ref[...])
    a = cb_ref[...]
    for j in range(FFN_CONV_TAPS):
        r0 = FFN_CARRY - (FFN_CONV_TAPS - 1) + j
        a = a + a_ref[r0:r0 + tm, :] * cw_ref[j:j + 1, :]
    o_ref[...] = (_silu(a) * u).astype(o_ref.dtype)
    a_ref[0:FFN_CARRY, :] = a_ref[tm:tm + FFN_CARRY, :]


def _ffn_up(h, wa, wu, conv_w, conv_b, seq):
    m, d = h.shape
    n = wa.shape[1]
    tm, tn = min(1024, seq), min(512, n)
    per_seq = seq // tm
    return pl.pallas_call(
        functools.partial(_ffn_up_kernel, per_seq),
        grid=(n // tn, m // tm),
        in_specs=[pl.BlockSpec((tm, d), lambda j, i: (i, 0)),
                  pl.BlockSpec((d, tn), lambda j, i: (0, j)),
                  pl.BlockSpec((d, tn), lambda j, i: (0, j)),
                  pl.BlockSpec((FFN_CONV_TAPS, tn), lambda j, i: (0, j)),
                  pl.BlockSpec((1, tn), lambda j, i: (0, j))],
        out_specs=pl.BlockSpec((tm, tn), lambda j, i: (i, j)),
        out_shape=jax.ShapeDtypeStruct((m, n), BF16),
        scratch_shapes=[pltpu.VMEM((tm + FFN_CARRY, tn), F32)],
        compiler_params=_params("arbitrary", "arbitrary"),
        name="ffn_up_conv_gate",
    )(h, wa, wu, conv_w, conv_b)


def _pad_cols(w, n):
    return jnp.pad(w, ((0, 0), (0, n - w.shape[1])))


def kernel(x, c, w_ada, b_ada, ada_table, mix_pre_norm, w_in, pool_w, pool_scale, pool_proj,
           dn_conv, dn_a_log, dn_dt_bias, dn_norm, dn_proj, w_o, mix_post_norm,
           ffn_pre_norm, w_up, ffn_conv, ffn_conv_b, w_down, ffn_post_norm):
    batch, seq, d = x.shape
    depth = w_in.shape[0]
    m = batch * seq
    pool_width = pool_scale.shape[1]
    dn_width = dn_proj.shape[1]
    heads = dn_a_log.shape[1]
    d_ff = w_down.shape[1]
    ff_tile = 512
    d_ff_pad = -(-d_ff // ff_tile) * ff_tile
    c_qkv = pool_width
    c_z = c_qkv + 3 * dn_width
    c_ba = c_z + dn_width
    c_gate = c_ba + 2 * heads

    c8 = jnp.pad(c, ((0, 8 - batch), (0, 0)))
    mod = _ada(c8, w_ada, b_ada, ada_table.reshape(depth, N_MOD * d))
    mod = mod[:, :batch].reshape(depth, batch, N_MOD, 1, d)

    xf = x.reshape(m, d)
    h = _norm_mod(xf, mix_pre_norm[0], mod[0, :, 1], mod[0, :, 0], seq)
    for l in range(depth):
        shift_f, scale_f, gate_m, gate_f = mod[l, :, 3], mod[l, :, 4], mod[l, :, 2], mod[l, :, 5]
        w_main = w_in[l, :, :c_ba].astype(BF16)
        w_ba = _pad_cols(w_in[l, :, c_ba:c_gate], LANES).astype(BF16)
        w_gates = w_in[l, :, c_gate:].astype(BF16)

        proj = _matmul(h, w_main, BF16, 1024, 512, "in_proj")
        gates = _ba(h, w_ba, dn_a_log[l], dn_dt_bias[l], heads)
        gates_t = gates[:, :2 * heads].T
        pa = _pool(proj, pool_w[l].astype(BF16), pool_scale[l], seq)
        q, k, v = _qkv_conv(proj, dn_conv[l], dn_width, c_qkv, seq)
        ob = _delta(q, k, v, proj, c_z, gates, gates_t, dn_norm[l], batch, seq, heads)
        y = _merge(h, pa, ob, w_gates, pool_proj[l].astype(BF16), dn_proj[l].astype(BF16))
        yo = _matmul(y, w_o[l].astype(BF16), F32, 1024, 512, "out_proj")
        xf, h = _resid_norm(yo, xf, mix_post_norm[l], gate_m, seq,
                            nxt=(ffn_pre_norm[l], scale_f, shift_f))

        wa = _pad_cols(w_up[l, :, :d_ff], d_ff_pad).astype(BF16)
        wu = _pad_cols(w_up[l, :, d_ff:], d_ff_pad).astype(BF16)
        cw = _pad_cols(ffn_conv[l], d_ff_pad)
        cb = _pad_cols(ffn_conv_b[l].reshape(1, d_ff), d_ff_pad)
        wd = jnp.pad(w_down[l], ((0, d_ff_pad - d_ff), (0, 0))).astype(BF16)
        act = _ffn_up(h, wa, wu, cw, cb, seq)
        yd = _matmul(act, wd, F32, 512, 512, "ffn_down")
        if l + 1 < depth:
            xf, h = _resid_norm(yd, xf, ffn_post_norm[l], gate_f, seq,
                                nxt=(mix_pre_norm[l + 1], mod[l + 1, :, 1], mod[l + 1, :, 0]))
        else:
            xf = _resid_norm(yd, xf, ffn_post_norm[l], gate_f, seq)
    return xf.reshape(batch, seq, d)
```

```python
import functools

import jax
import jax.numpy as jnp
from jax import lax
from jax.experimental import pallas as pl
from jax.experimental.pallas import tpu as pltpu

F32 = jnp.float32
BF16 = jnp.bfloat16

EPS = 1e-6
POOL_WINDOWS = (2, 4, 8, 16)
HEAD_DIM = 128
DN_CONV_TAPS = 4
FFN_CONV_TAPS = 3
N_MOD = 6
CHUNK = 128
HALO = 16
VMEM_LIMIT_BYTES = 56 * 1024 * 1024
LANES = 128


def _params(*semantics):
    return pltpu.CompilerParams(dimension_semantics=semantics,
                                vmem_limit_bytes=VMEM_LIMIT_BYTES)


def _dot(a, b):
    return jnp.dot(a, b, preferred_element_type=F32)


def _dot_nt(a, b):
    return lax.dot_general(a, b, (((1,), (1,)), ((), ())), preferred_element_type=F32)


def _sigmoid(x):
    return 1.0 / (1.0 + jnp.exp(-x))


def _silu(x):
    return x * _sigmoid(x)


def _softplus(x):
    return jnp.maximum(x, 0.0) + jnp.log1p(jnp.exp(-jnp.abs(x)))


def _rms(x):
    return x * lax.rsqrt(jnp.mean(x * x, axis=-1, keepdims=True) + EPS)


def _ada_kernel(c_ref, w_ref, b_ref, t_ref, o_ref):
    a = _silu(c_ref[...]).astype(BF16)
    acc = _dot(a, w_ref[...].astype(BF16)) + b_ref[...]
    o_ref[...] = acc[None] + t_ref[...]


def _ada(c8, w_ada, b_ada, table):
    depth = table.shape[0]
    d, n = w_ada.shape
    tn = min(512, n)
    return pl.pallas_call(
        _ada_kernel,
        grid=(n // tn,),
        in_specs=[pl.BlockSpec((8, d), lambda j: (0, 0)),
                  pl.BlockSpec((d, tn), lambda j: (0, j)),
                  pl.BlockSpec((1, tn), lambda j: (0, j)),
                  pl.BlockSpec((depth, 1, tn), lambda j: (0, 0, j))],
        out_specs=pl.BlockSpec((depth, 8, tn), lambda j: (0, 0, j)),
        out_shape=jax.ShapeDtypeStruct((depth, 8, n), F32),
        compiler_params=_params("arbitrary"),
        name="ada",
    )(c8, w_ada, b_ada.reshape(1, n), table.reshape(depth, 1, n))


def _norm_mod_kernel(x_ref, w_ref, sc_ref, sh_ref, h_ref):
    y = _rms(x_ref[...]) * w_ref[...]
    h_ref[...] = (y * (1.0 + sc_ref[0]) + sh_ref[0]).astype(h_ref.dtype)


def _norm_mod(x, w, scale, shift, seq):
    m, d = x.shape
    tm = min(256, seq)
    per_seq = seq // tm
    row = pl.BlockSpec((tm, d), lambda i: (i, 0))
    vec = pl.BlockSpec((1, d), lambda i: (0, 0))
    bvec = pl.BlockSpec((1, 1, d), lambda i: (i // per_seq, 0, 0))
    return pl.pallas_call(
        _norm_mod_kernel,
        grid=(m // tm,),
        in_specs=[row, vec, bvec, bvec],
        out_specs=row,
        out_shape=jax.ShapeDtypeStruct((m, d), BF16),
        compiler_params=_params("parallel"),
        name="norm_mod",
    )(x, w.reshape(1, d), scale, shift)


def _resid_norm_kernel(y_ref, x_ref, pw_ref, g_ref, nw_ref, sc_ref, sh_ref, xo_ref, h_ref):
    xn = x_ref[...] + g_ref[0] * (_rms(y_ref[...].astype(F32)) * pw_ref[...])
    xo_ref[...] = xn
    h = _rms(xn) * nw_ref[...]
    h_ref[...] = (h * (1.0 + sc_ref[0]) + sh_ref[0]).astype(h_ref.dtype)


def _resid_kernel(y_ref, x_ref, pw_ref, g_ref, xo_ref):
    xo_ref[...] = x_ref[...] + g_ref[0] * (_rms(y_ref[...].astype(F32)) * pw_ref[...])


def _resid_norm(y, x, post_w, gate, seq, nxt=None):
    m, d = x.shape
    tm = min(256, seq)
    per_seq = seq // tm
    row = pl.BlockSpec((tm, d), lambda i: (i, 0))
    vec = pl.BlockSpec((1, d), lambda i: (0, 0))
    bvec = pl.BlockSpec((1, 1, d), lambda i: (i // per_seq, 0, 0))
    if nxt is None:
        return pl.pallas_call(
            _resid_kernel,
            grid=(m // tm,),
            in_specs=[row, row, vec, bvec],
            out_specs=row,
            out_shape=jax.ShapeDtypeStruct((m, d), F32),
            compiler_params=_params("parallel"),
            name="resid",
        )(y, x, post_w.reshape(1, d), gate)
    norm_w, scale, shift = nxt
    return pl.pallas_call(
        _resid_norm_kernel,
        grid=(m // tm,),
        in_specs=[row, row, vec, bvec, vec, bvec, bvec],
        out_specs=[row, row],
        out_shape=[jax.ShapeDtypeStruct((m, d), F32), jax.ShapeDtypeStruct((m, d), BF16)],
        compiler_params=_params("parallel"),
        name="resid_norm",
    )(y, x, post_w.reshape(1, d), gate, norm_w.reshape(1, d), scale, shift)


def _mm_kernel(a_ref, w_ref, o_ref):
    o_ref[...] = _dot(a_ref[...], w_ref[...]).astype(o_ref.dtype)


def _matmul(a, w, layer, n, out_dtype, tm, tn, name):
    m, k = a.shape
    tm, tn = min(tm, m), min(tn, n)
    return pl.pallas_call(
        _mm_kernel,
        grid=(n // tn, m // tm),
        in_specs=[pl.BlockSpec((tm, k), lambda j, i: (i, 0)),
                  pl.BlockSpec((None, k, tn), lambda j, i: (layer, 0, j))],
        out_specs=pl.BlockSpec((tm, tn), lambda j, i: (i, j)),
        out_shape=jax.ShapeDtypeStruct((m, n), out_dtype),
        compiler_params=_params("parallel", "arbitrary"),
        name=name,
    )(a, w)


def _ba_kernel(heads, h_ref, w_ref, alog_ref, dtb_ref, o_ref):
    p = _dot(h_ref[...], w_ref[...])
    lane = lax.broadcasted_iota(jnp.int32, p.shape, 1)
    is_g = (lane >= heads) & (lane < 2 * heads)
    g = jnp.where(is_g, -jnp.exp(alog_ref[...]) * _softplus(p + dtb_ref[...]), 0.0)
    pos = lax.broadcasted_iota(jnp.int32, p.shape, 0) % CHUNK
    shift = 1
    while shift < CHUNK:
        g = g + jnp.where(pos >= shift, pltpu.roll(g, shift, axis=0), 0.0)
        shift *= 2
    o_ref[...] = jnp.where(is_g, g, _sigmoid(p))


def _ba(h, w, layer, col0, a_log, dt_bias, heads):
    m, k = h.shape
    tm = min(512, m)
    assert col0 % LANES == 0 and 2 * heads <= LANES
    pad = lambda v: jnp.zeros((1, LANES), F32).at[0, heads:2 * heads].set(v)
    return pl.pallas_call(
        functools.partial(_ba_kernel, heads),
        grid=(m // tm,),
        in_specs=[pl.BlockSpec((tm, k), lambda i: (i, 0)),
                  pl.BlockSpec((None, k, LANES), lambda i: (layer, 0, col0 // LANES)),
                  pl.BlockSpec((1, LANES), lambda i: (0, 0)),
                  pl.BlockSpec((1, LANES), lambda i: (0, 0))],
        out_specs=pl.BlockSpec((tm, LANES), lambda i: (i, 0)),
        out_shape=jax.ShapeDtypeStruct((m, LANES), F32),
        compiler_params=_params("parallel"),
        name="dn_gates",
    )(h, w, pad(a_log), pad(dt_bias))


def _pool_kernel(per_seq, u_ref, halo_ref, pw_ref, sc_ref, o_ref, buf_ref, pa_ref):
    tm, width = u_ref.shape
    group = width // len(POOL_WINDOWS)
    t = pl.program_id(0) % per_seq
    halo = halo_ref[...].astype(F32)
    buf_ref[0:HALO, :] = jnp.where(t == 0, 0.0, halo)
    buf_ref[HALO:, :] = u_ref[...].astype(F32)
    pos = t * tm + lax.broadcasted_iota(jnp.int32, (tm, LANES), 0)
    for gi, win in enumerate(POOL_WINDOWS):
        inv_count = 1.0 / jnp.minimum(pos + 1, win).astype(F32)
        for c in range(gi * group, (gi + 1) * group, LANES):
            tok = buf_ref[HALO:HALO + tm, c:c + LANES]
            acc = tok
            for j in range(1, win):
                acc = acc + buf_ref[HALO - j:HALO - j + tm, c:c + LANES]
            pa_ref[:, c:c + LANES] = (acc * inv_count - tok).astype(BF16)
    for gi in range(len(POOL_WINDOWS)):
        cols = slice(gi * group, (gi + 1) * group)
        r = _dot(pa_ref[:, cols], pw_ref[gi]) * sc_ref[:, cols]
        o_ref[:, cols] = r.astype(o_ref.dtype)


def _pool(proj, pool_w, layer, pool_scale, seq):
    m = proj.shape[0]
    _, groups, group, _ = pool_w.shape
    width = groups * group
    tm = min(256, seq)
    per_seq = seq // tm
    return pl.pallas_call(
        functools.partial(_pool_kernel, per_seq),
        grid=(m // tm,),
        in_specs=[pl.BlockSpec((tm, width), lambda i: (i, 0)),
                  pl.BlockSpec((HALO, width), lambda i: (jnp.maximum(i * (tm // HALO) - 1, 0), 0)),
                  pl.BlockSpec((None, groups, group, group), lambda i: (layer, 0, 0, 0)),
                  pl.BlockSpec((1, width), lambda i: (0, 0))],
        out_specs=pl.BlockSpec((tm, width), lambda i: (i, 0)),
        out_shape=jax.ShapeDtypeStruct((m, width), BF16),
        scratch_shapes=[pltpu.VMEM((tm + HALO, width), F32), pltpu.VMEM((tm, width), BF16)],
        compiler_params=_params("parallel"),
        name="pool_mixer",
    )(proj, proj, pool_w, pool_scale.reshape(1, width))


def _qkv_kernel(per_seq, q_ref, k_ref, v_ref, hq_ref, hk_ref, hv_ref, wq_ref, wk_ref, wv_ref,
                qo_ref, ko_ref, vo_ref, buf_ref):
    tm, width = q_ref.shape
    t = pl.program_id(0) % per_seq
    first = t == 0
    for x_ref, halo_ref, w_ref, o_ref, norm in (
            (q_ref, hq_ref, wq_ref, qo_ref, HEAD_DIM ** -0.5),
            (k_ref, hk_ref, wk_ref, ko_ref, 1.0),
            (v_ref, hv_ref, wv_ref, vo_ref, None)):
        buf_ref[0:HALO, :] = jnp.where(first, 0.0, halo_ref[...].astype(F32))
        buf_ref[HALO:, :] = x_ref[...].astype(F32)
        for c in range(0, width, HEAD_DIM):
            cols = slice(c, c + HEAD_DIM)
            y = None
            for j in range(DN_CONV_TAPS):
                r0 = HALO - (DN_CONV_TAPS - 1) + j
                term = buf_ref[r0:r0 + tm, cols] * w_ref[j:j + 1, cols]
                y = term if y is None else y + term
            y = _silu(y)
            if norm is not None:
                y = y * lax.rsqrt(jnp.sum(y * y, axis=-1, keepdims=True) + EPS) * norm
            o_ref[:, cols] = y.astype(o_ref.dtype)


def _qkv_conv(proj, dn_conv, width, col0, seq):
    m = proj.shape[0]
    tm = min(128, seq)
    per_seq = seq // tm
    b0 = col0 // width
    rows = [pl.BlockSpec((tm, width), functools.partial(lambda s, i: (i, b0 + s), s)) for s in range(3)]
    halos = [pl.BlockSpec((HALO, width),
                          functools.partial(lambda s, i: (jnp.maximum(i * (tm // HALO) - 1, 0), b0 + s), s))
             for s in range(3)]
    taps = [pl.BlockSpec((DN_CONV_TAPS, width), functools.partial(lambda s, i: (0, s), s)) for s in range(3)]
    out = pl.BlockSpec((tm, width), lambda i: (i, 0))
    shp = jax.ShapeDtypeStruct((m, width), BF16)
    return pl.pallas_call(
        functools.partial(_qkv_kernel, per_seq),
        grid=(m // tm,),
        in_specs=rows + halos + taps,
        out_specs=[out, out, out],
        out_shape=[shp, shp, shp],
        scratch_shapes=[pltpu.VMEM((tm + HALO, width), F32)],
        compiler_params=_params("parallel"),
        name="dn_qkv_conv",
    )(proj, proj, proj, proj, proj, proj, dn_conv, dn_conv, dn_conv)


def _unit_lower_inverses(lows):
    n = lows[0].shape[0]
    ri = lax.broadcasted_iota(jnp.int32, (n, n), 0)
    ci = lax.broadcasted_iota(jnp.int32, (n, n), 1)
    same = lambda bits: jnp.right_shift(ri, bits) == jnp.right_shift(ci, bits)
    eye = jnp.where(ri == ci, 1.0, 0.0)
    invs = [eye - jnp.where(same(1), low, 0.0) for low in lows]
    bits = 1
    while (1 << bits) < n:
        mask = same(bits + 1) & jnp.logical_not(same(bits))
        inv16 = [inv.astype(BF16) for inv in invs]
        right = [_dot(jnp.where(mask, low, 0.0).astype(BF16), i16) for low, i16 in zip(lows, inv16)]
        invs = [inv - _dot(i16, r.astype(BF16)) for inv, i16, r in zip(invs, inv16, right)]
        bits += 1
    return invs


def _delta_kernel(heads, q_ref, k_ref, v_ref, z_ref, g_ref, gt_ref, nw_ref, o_ref, state_ref):
    rows, width = q_ref.shape
    hp = width // HEAD_DIM
    c = CHUNK

    @pl.when(pl.program_id(2) == 0)
    def _():
        state_ref[...] = jnp.zeros_like(state_ref)

    gates = g_ref[...]
    lane = lax.broadcasted_iota(jnp.int32, gates.shape, 1)
    ri = lax.broadcasted_iota(jnp.int32, (c, c), 0)
    ci = lax.broadcasted_iota(jnp.int32, (c, c), 1)
    causal = ri >= ci
    strict = ri > ci

    work = []
    for hh in range(hp):
        head = pl.program_id(1) * hp + hh
        cols = slice(hh * HEAD_DIM, (hh + 1) * HEAD_DIM)
        beta_col = jnp.sum(jnp.where(lane == head, gates, 0.0), axis=-1, keepdims=True)
        gc_col = jnp.sum(jnp.where(lane == heads + head, gates, 0.0), axis=-1, keepdims=True)
        gc_row = gt_ref[pl.ds(heads + head, 1), :]
        for r0 in range(0, rows, c):
            rsl = slice(r0, r0 + c)
            q = q_ref[rsl, cols]
            k16 = k_ref[rsl, cols]
            k = k16.astype(F32)
            beta = beta_col[rsl]
            gcc = gc_col[rsl]
            gcr = gc_row[:, rsl]
            g_last = gcr[:, c - 1:c]
            decay = jnp.exp(jnp.where(causal, gcc - gcr, -jnp.inf))
            e_g = jnp.exp(gcc)
            kb = k * beta
            kq_k = _dot_nt(jnp.concatenate([kb.astype(BF16), q], axis=0), k16)
            low = jnp.where(strict, kq_k[:c] * decay, 0.0)
            qk = jnp.where(causal, kq_k[c:] * decay, 0.0)
            work.append(dict(
                hh=hh, rsl=rsl, cols=cols, low=low,
                vb_kbg16=jnp.concatenate([(v_ref[rsl, cols].astype(F32) * beta).astype(BF16),
                                          (kb * e_g).astype(BF16)], axis=1),
                qd16=(q.astype(F32) * e_g).astype(BF16),
                qk_kdt16=jnp.concatenate([qk.astype(BF16),
                                          (k * jnp.exp(g_last - gcc)).T.astype(BF16)], axis=0),
                g_end=jnp.exp(g_last)))
    for item, t_inv in zip(work, _unit_lower_inverses([item["low"] for item in work])):
        uw = _dot(t_inv.astype(BF16), item["vb_kbg16"])
        item["u"] = uw[:, :HEAD_DIM]
        item["w_qd16"] = jnp.concatenate([uw[:, HEAD_DIM:].astype(BF16), item["qd16"]], axis=0)

    for hh in range(hp):
        state = state_ref[hh]
        for item in work:
            if item["hh"] != hh:
                continue
            from_state = _dot(item["w_qd16"], state.astype(BF16))
            vn16 = (item["u"] - from_state[:c]).astype(BF16)
            from_new = _dot(item["qk_kdt16"], vn16)
            item["o"] = from_state[c:] + from_new[:c]
            state = state * item["g_end"] + from_new[c:]
        state_ref[hh] = state

    for item in work:
        rsl, cols = item["rsl"], item["cols"]
        z = z_ref[rsl, cols].astype(F32)
        o_ref[rsl, cols] = (_rms(item["o"]) * nw_ref[...] * _silu(z)).astype(o_ref.dtype)


def _delta(q, k, v, proj, z_col0, gates, gates_t, dn_norm, batch, seq, heads):
    m, width = q.shape
    hp = min(4, heads)
    rows = min(2 * CHUNK, seq)
    per_seq = seq // rows
    bw = hp * HEAD_DIM
    zb0 = z_col0 // bw
    qkv = pl.BlockSpec((rows, bw), lambda b, h, t: (b * per_seq + t, h))
    return pl.pallas_call(
        functools.partial(_delta_kernel, heads),
        grid=(batch, heads // hp, per_seq),
        in_specs=[qkv, qkv, qkv,
                  pl.BlockSpec((rows, bw), lambda b, h, t: (b * per_seq + t, zb0 + h)),
                  pl.BlockSpec((rows, LANES), lambda b, h, t: (b * per_seq + t, 0)),
                  pl.BlockSpec((2 * heads, rows), lambda b, h, t: (0, b * per_seq + t)),
                  pl.BlockSpec((1, HEAD_DIM), lambda b, h, t: (0, 0))],
        out_specs=qkv,
        out_shape=jax.ShapeDtypeStruct((m, width), BF16),
        scratch_shapes=[pltpu.VMEM((hp, HEAD_DIM, HEAD_DIM), F32)],
        compiler_params=_params("parallel", "parallel", "arbitrary"),
        name="gated_delta",
    )(q, k, v, proj, gates, gates_t, dn_norm.reshape(1, HEAD_DIM))


def _merge_kernel(h_ref, pa_ref, ob_ref, wga_ref, wgb_ref, pp_ref, dp_ref, y_ref):
    h = h_ref[...]
    ya = _sigmoid(_dot(h, wga_ref[...])) * _dot(pa_ref[...], pp_ref[...])
    yb = _sigmoid(_dot(h, wgb_ref[...])) * _dot(ob_ref[...], dp_ref[...])
    y_ref[...] = (ya + yb).astype(y_ref.dtype)


def _merge(h, pa, ob, w_gates, pool_proj, dn_proj, layer):
    m, d = h.shape
    kp = pa.shape[1]
    tm, tn = min(512, m), min(512, d)
    nb = d // tn
    return pl.pallas_call(
        _merge_kernel,
        grid=(nb, m // tm),
        in_specs=[pl.BlockSpec((tm, d), lambda j, i: (i, 0)),
                  pl.BlockSpec((tm, kp), lambda j, i: (i, 0)),
                  pl.BlockSpec((tm, kp), lambda j, i: (i, 0)),
                  pl.BlockSpec((None, d, tn), lambda j, i: (layer, 0, j)),
                  pl.BlockSpec((None, d, tn), lambda j, i: (layer, 0, nb + j)),
                  pl.BlockSpec((None, kp, tn), lambda j, i: (layer, 0, j)),
                  pl.BlockSpec((None, kp, tn), lambda j, i: (layer, 0, j))],
        out_specs=pl.BlockSpec((tm, tn), lambda j, i: (i, j)),
        out_shape=jax.ShapeDtypeStruct((m, d), BF16),
        compiler_params=_params("parallel", "arbitrary"),
        name="gated_merge",
    )(h, pa, ob, w_gates, w_gates, pool_proj, dn_proj)


FFN_CARRY = 8


def _ffn_up_kernel(per_seq, h_ref, wa_ref, wu_ref, cw_ref, cb_ref, o_ref, a_ref):
    tm = h_ref.shape[0]

    @pl.when(pl.program_id(1) % per_seq == 0)
    def _():
        a_ref[0:FFN_CARRY, :] = jnp.zeros((FFN_CARRY, a_ref.shape[1]), F32)

    h = h_ref[...]
    a_ref[FFN_CARRY:, :] = _dot(h, wa_ref[...])
    u = _dot(h, wu_ref[...])
    a = cb_ref[...]
    for j in range(FFN_CONV_TAPS):
        r0 = FFN_CARRY - (FFN_CONV_TAPS - 1) + j
        a = a + a_ref[r0:r0 + tm, :] * cw_ref[j:j + 1, :]
    o_ref[...] = (_silu(a) * u).astype(o_ref.dtype)
    a_ref[0:FFN_CARRY, :] = a_ref[tm:tm + FFN_CARRY, :]


def _ffn_up(h, w_up, layer, conv_w, conv_b, seq):
    m, d = h.shape
    n = w_up.shape[2] // 2
    tm, tn = min(2048, seq), 256
    assert n % tn == 0
    nb = n // tn
    per_seq = seq // tm
    return pl.pallas_call(
        functools.partial(_ffn_up_kernel, per_seq),
        grid=(n // tn, m // tm),
        in_specs=[pl.BlockSpec((tm, d), lambda j, i: (i, 0)),
                  pl.BlockSpec((None, d, tn), lambda j, i: (layer, 0, j)),
                  pl.BlockSpec((None, d, tn), lambda j, i: (layer, 0, nb + j)),
                  pl.BlockSpec((None, FFN_CONV_TAPS, tn), lambda j, i: (layer, 0, j)),
                  pl.BlockSpec((None, 1, tn), lambda j, i: (layer, 0, j))],
        out_specs=pl.BlockSpec((tm, tn), lambda j, i: (i, j)),
        out_shape=jax.ShapeDtypeStruct((m, n), BF16),
        scratch_shapes=[pltpu.VMEM((tm + FFN_CARRY, tn), F32)],
        compiler_params=_params("arbitrary", "arbitrary"),
        name="ffn_up_conv_gate",
    )(h, w_up, w_up, conv_w, conv_b.reshape(conv_b.shape[0], 1, n))


def kernel(x, c, w_ada, b_ada, ada_table, mix_pre_norm, w_in, pool_w, pool_scale, pool_proj,
           dn_conv, dn_a_log, dn_dt_bias, dn_norm, dn_proj, w_o, mix_post_norm,
           ffn_pre_norm, w_up, ffn_conv, ffn_conv_b, w_down, ffn_post_norm):
    batch, seq, d = x.shape
    depth = w_in.shape[0]
    m = batch * seq
    pool_width = pool_scale.shape[1]
    dn_width = dn_proj.shape[1]
    heads = dn_a_log.shape[1]
    c_qkv = pool_width
    c_z = c_qkv + 3 * dn_width
    c_ba = c_z + dn_width
    c_gate = c_ba + 2 * heads

    c8 = jnp.pad(c, ((0, 8 - batch), (0, 0)))
    mod = _ada(c8, w_ada, b_ada, ada_table.reshape(depth, N_MOD * d))
    mod = mod[:, :batch].reshape(depth, batch, N_MOD, 1, d)

    xf = x.reshape(m, d)
    h = _norm_mod(xf, mix_pre_norm[0], mod[0, :, 1], mod[0, :, 0], seq)
    w_in16 = w_in[:, :, :c_ba + LANES].astype(BF16)
    w_gates16 = w_in[:, :, c_gate:].astype(BF16)
    pool_w16, pool_proj16, dn_proj16 = pool_w.astype(BF16), pool_proj.astype(BF16), dn_proj.astype(BF16)
    w_o16, w_up16, w_down16 = w_o.astype(BF16), w_up.astype(BF16), w_down.astype(BF16)
    for l in range(depth):
        shift_f, scale_f, gate_m, gate_f = mod[l, :, 3], mod[l, :, 4], mod[l, :, 2], mod[l, :, 5]
        proj = _matmul(h, w_in16, l, c_ba, BF16, 1024, 512, "in_proj")
        gates = _ba(h, w_in16, l, c_ba, dn_a_log[l], dn_dt_bias[l], heads)
        gates_t = gates[:, :2 * heads].T
        pa = _pool(proj, pool_w16, l, pool_scale[l], seq)
        q, k, v = _qkv_conv(proj, dn_conv[l], dn_width, c_qkv, seq)
        ob = _delta(q, k, v, proj, c_z, gates, gates_t, dn_norm[l], batch, seq, heads)
        y = _merge(h, pa, ob, w_gates16, pool_proj16, dn_proj16, l)
        yo = _matmul(y, w_o16, l, d, F32, 1024, 512, "out_proj")
        xf, h = _resid_norm(yo, xf, mix_post_norm[l], gate_m, seq,
                            nxt=(ffn_pre_norm[l], scale_f, shift_f))
        act = _ffn_up(h, w_up16, l, ffn_conv, ffn_conv_b, seq)
        yd = _matmul(act, w_down16, l, d, F32, 512, 512, "ffn_down")
        if l + 1 < depth:
            xf, h = _resid_norm(yd, xf, ffn_post_norm[l], gate_f, seq,
                                nxt=(mix_pre_norm[l + 1], mod[l + 1, :, 1], mod[l + 1, :, 0]))
        else:
            xf = _resid_norm(yd, xf, ffn_post_norm[l], gate_f, seq)
    return xf.reshape(batch, seq, d)
```

```python
import functools

import jax
import jax.numpy as jnp
from jax import lax
from jax.experimental import pallas as pl
from jax.experimental.pallas import tpu as pltpu

F32 = jnp.float32
BF16 = jnp.bfloat16

EPS = 1e-6
POOL_WINDOWS = (2, 4, 8, 16)
HEAD_DIM = 128
DN_CONV_TAPS = 4
FFN_CONV_TAPS = 3
N_MOD = 6
CHUNK = 128
HALO = 16
VMEM_LIMIT_BYTES = 56 * 1024 * 1024
LANES = 128


def _params(*semantics):
    return pltpu.CompilerParams(dimension_semantics=semantics,
                                vmem_limit_bytes=VMEM_LIMIT_BYTES)


def _dot(a, b):
    return jnp.dot(a, b, preferred_element_type=F32)


def _dot_nt(a, b):
    return lax.dot_general(a, b, (((1,), (1,)), ((), ())), preferred_element_type=F32)


def _sigmoid(x):
    return 1.0 / (1.0 + jnp.exp(-x))


def _silu(x):
    return x * _sigmoid(x)


def _softplus(x):
    return jnp.maximum(x, 0.0) + jnp.log1p(jnp.exp(-jnp.abs(x)))


def _rms(x):
    return x * lax.rsqrt(jnp.mean(x * x, axis=-1, keepdims=True) + EPS)


def _ada_kernel(c_ref, w_ref, b_ref, t_ref, o_ref):
    a = _silu(c_ref[...]).astype(BF16)
    acc = _dot(a, w_ref[...].astype(BF16)) + b_ref[...]
    o_ref[...] = acc[None] + t_ref[...]


def _ada(c8, w_ada, b_ada, table):
    depth = table.shape[0]
    d, n = w_ada.shape
    tn = min(512, n)
    return pl.pallas_call(
        _ada_kernel,
        grid=(n // tn,),
        in_specs=[pl.BlockSpec((8, d), lambda j: (0, 0)),
                  pl.BlockSpec((d, tn), lambda j: (0, j)),
                  pl.BlockSpec((1, tn), lambda j: (0, j)),
                  pl.BlockSpec((depth, 1, tn), lambda j: (0, 0, j))],
        out_specs=pl.BlockSpec((depth, 8, tn), lambda j: (0, 0, j)),
        out_shape=jax.ShapeDtypeStruct((depth, 8, n), F32),
        compiler_params=_params("arbitrary"),
        name="ada",
    )(c8, w_ada, b_ada.reshape(1, n), table.reshape(depth, 1, n))


def _norm_mod_kernel(x_ref, w_ref, sc_ref, sh_ref, h_ref):
    y = _rms(x_ref[...]) * w_ref[...]
    h_ref[...] = (y * (1.0 + sc_ref[0]) + sh_ref[0]).astype(h_ref.dtype)


def _norm_mod(x, w, scale, shift, seq):
    m, d = x.shape
    tm = min(256, seq)
    per_seq = seq // tm
    row = pl.BlockSpec((tm, d), lambda i: (i, 0))
    vec = pl.BlockSpec((1, d), lambda i: (0, 0))
    bvec = pl.BlockSpec((1, 1, d), lambda i: (i // per_seq, 0, 0))
    return pl.pallas_call(
        _norm_mod_kernel,
        grid=(m // tm,),
        in_specs=[row, vec, bvec, bvec],
        out_specs=row,
        out_shape=jax.ShapeDtypeStruct((m, d), BF16),
        compiler_params=_params("parallel"),
        name="norm_mod",
    )(x, w.reshape(1, d), scale, shift)


def _resid_norm_kernel(y_ref, x_ref, pw_ref, g_ref, nw_ref, sc_ref, sh_ref, xo_ref, h_ref):
    xn = x_ref[...] + g_ref[0] * (_rms(y_ref[...].astype(F32)) * pw_ref[...])
    xo_ref[...] = xn
    h = _rms(xn) * nw_ref[...]
    h_ref[...] = (h * (1.0 + sc_ref[0]) + sh_ref[0]).astype(h_ref.dtype)


def _resid_kernel(y_ref, x_ref, pw_ref, g_ref, xo_ref):
    xo_ref[...] = x_ref[...] + g_ref[0] * (_rms(y_ref[...].astype(F32)) * pw_ref[...])


def _resid_norm(y, x, post_w, gate, seq, nxt=None):
    m, d = x.shape
    tm = min(256, seq)
    per_seq = seq // tm
    row = pl.BlockSpec((tm, d), lambda i: (i, 0))
    vec = pl.BlockSpec((1, d), lambda i: (0, 0))
    bvec = pl.BlockSpec((1, 1, d), lambda i: (i // per_seq, 0, 0))
    if nxt is None:
        return pl.pallas_call(
            _resid_kernel,
            grid=(m // tm,),
            in_specs=[row, row, vec, bvec],
            out_specs=row,
            out_shape=jax.ShapeDtypeStruct((m, d), F32),
            compiler_params=_params("parallel"),
            name="resid",
        )(y, x, post_w.reshape(1, d), gate)
    norm_w, scale, shift = nxt
    return pl.pallas_call(
        _resid_norm_kernel,
        grid=(m // tm,),
        in_specs=[row, row, vec, bvec, vec, bvec, bvec],
        out_specs=[row, row],
        out_shape=[jax.ShapeDtypeStruct((m, d), F32), jax.ShapeDtypeStruct((m, d), BF16)],
        compiler_params=_params("parallel"),
        name="resid_norm",
    )(y, x, post_w.reshape(1, d), gate, norm_w.reshape(1, d), scale, shift)


def _mm_kernel(a_ref, w_ref, o_ref):
    o_ref[...] = _dot(a_ref[...], w_ref[...].astype(BF16)).astype(o_ref.dtype)


def _matmul(a, w, layer, n, out_dtype, tm, tn, name):
    m, k = a.shape
    tm, tn = min(tm, m), min(tn, n)
    return pl.pallas_call(
        _mm_kernel,
        grid=(n // tn, m // tm),
        in_specs=[pl.BlockSpec((tm, k), lambda j, i: (i, 0)),
                  pl.BlockSpec((None, k, tn), lambda j, i: (layer, 0, j))],
        out_specs=pl.BlockSpec((tm, tn), lambda j, i: (i, j)),
        out_shape=jax.ShapeDtypeStruct((m, n), out_dtype),
        compiler_params=_params("parallel", "arbitrary"),
        name=name,
    )(a, w)


def _ba_kernel(heads, h_ref, w_ref, alog_ref, dtb_ref, o_ref):
    p = _dot(h_ref[...], w_ref[...].astype(BF16))
    lane = lax.broadcasted_iota(jnp.int32, p.shape, 1)
    is_g = (lane >= heads) & (lane < 2 * heads)
    g = jnp.where(is_g, -jnp.exp(alog_ref[...]) * _softplus(p + dtb_ref[...]), 0.0)
    pos = lax.broadcasted_iota(jnp.int32, p.shape, 0) % CHUNK
    shift = 1
    while shift < CHUNK:
        g = g + jnp.where(pos >= shift, pltpu.roll(g, shift, axis=0), 0.0)
        shift *= 2
    o_ref[...] = jnp.where(is_g, g, _sigmoid(p))


def _ba(h, w, layer, col0, a_log, dt_bias, heads):
    m, k = h.shape
    tm = min(512, m)
    assert col0 % LANES == 0 and 2 * heads <= LANES
    pad = lambda v: jnp.zeros((1, LANES), F32).at[0, heads:2 * heads].set(v)
    return pl.pallas_call(
        functools.partial(_ba_kernel, heads),
        grid=(m // tm,),
        in_specs=[pl.BlockSpec((tm, k), lambda i: (i, 0)),
                  pl.BlockSpec((None, k, LANES), lambda i: (layer, 0, col0 // LANES)),
                  pl.BlockSpec((1, LANES), lambda i: (0, 0)),
                  pl.BlockSpec((1, LANES), lambda i: (0, 0))],
        out_specs=pl.BlockSpec((tm, LANES), lambda i: (i, 0)),
        out_shape=jax.ShapeDtypeStruct((m, LANES), F32),
        compiler_params=_params("parallel"),
        name="dn_gates",
    )(h, w, pad(a_log), pad(dt_bias))


def _pool_kernel(per_seq, u_ref, halo_ref, pw_ref, sc_ref, o_ref, buf_ref, pa_ref):
    tm, width = u_ref.shape
    group = width // len(POOL_WINDOWS)
    t = pl.program_id(0) % per_seq
    halo = halo_ref[...].astype(F32)
    buf_ref[0:HALO, :] = jnp.where(t == 0, 0.0, halo)
    buf_ref[HALO:, :] = u_ref[...].astype(F32)
    pos = t * tm + lax.broadcasted_iota(jnp.int32, (tm, LANES), 0)
    for gi, win in enumerate(POOL_WINDOWS):
        inv_count = 1.0 / jnp.minimum(pos + 1, win).astype(F32)
        for c in range(gi * group, (gi + 1) * group, LANES):
            tok = buf_ref[HALO:HALO + tm, c:c + LANES]
            acc = tok
            for j in range(1, win):
                acc = acc + buf_ref[HALO - j:HALO - j + tm, c:c + LANES]
            pa_ref[:, c:c + LANES] = (acc * inv_count - tok).astype(BF16)
    for gi in range(len(POOL_WINDOWS)):
        cols = slice(gi * group, (gi + 1) * group)
        r = _dot(pa_ref[:, cols], pw_ref[gi].astype(BF16)) * sc_ref[:, cols]
        o_ref[:, cols] = r.astype(o_ref.dtype)


def _pool(proj, pool_w, layer, pool_scale, seq):
    m = proj.shape[0]
    _, groups, group, _ = pool_w.shape
    width = groups * group
    tm = min(256, seq)
    per_seq = seq // tm
    return pl.pallas_call(
        functools.partial(_pool_kernel, per_seq),
        grid=(m // tm,),
        in_specs=[pl.BlockSpec((tm, width), lambda i: (i, 0)),
                  pl.BlockSpec((HALO, width), lambda i: (jnp.maximum(i * (tm // HALO) - 1, 0), 0)),
                  pl.BlockSpec((None, groups, group, group), lambda i: (layer, 0, 0, 0)),
                  pl.BlockSpec((1, width), lambda i: (0, 0))],
        out_specs=pl.BlockSpec((tm, width), lambda i: (i, 0)),
        out_shape=jax.ShapeDtypeStruct((m, width), BF16),
        scratch_shapes=[pltpu.VMEM((tm + HALO, width), F32), pltpu.VMEM((tm, width), BF16)],
        compiler_params=_params("parallel"),
        name="pool_mixer",
    )(proj, proj, pool_w, pool_scale.reshape(1, width))


def _qkv_kernel(per_seq, q_ref, k_ref, v_ref, hq_ref, hk_ref, hv_ref, wq_ref, wk_ref, wv_ref,
                qo_ref, ko_ref, vo_ref, buf_ref):
    tm, width = q_ref.shape
    t = pl.program_id(0) % per_seq
    first = t == 0
    for x_ref, halo_ref, w_ref, o_ref, norm in (
            (q_ref, hq_ref, wq_ref, qo_ref, HEAD_DIM ** -0.5),
            (k_ref, hk_ref, wk_ref, ko_ref, 1.0),
            (v_ref, hv_ref, wv_ref, vo_ref, None)):
        buf_ref[0:HALO, :] = jnp.where(first, 0.0, halo_ref[...].astype(F32))
        buf_ref[HALO:, :] = x_ref[...].astype(F32)
        for c in range(0, width, HEAD_DIM):
            cols = slice(c, c + HEAD_DIM)
            y = None
            for j in range(DN_CONV_TAPS):
                r0 = HALO - (DN_CONV_TAPS - 1) + j
                term = buf_ref[r0:r0 + tm, cols] * w_ref[j:j + 1, cols]
                y = term if y is None else y + term
            y = _silu(y)
            if norm is not None:
                y = y * lax.rsqrt(jnp.sum(y * y, axis=-1, keepdims=True) + EPS) * norm
            o_ref[:, cols] = y.astype(o_ref.dtype)


def _qkv_conv(proj, dn_conv, width, col0, seq):
    m = proj.shape[0]
    tm = min(128, seq)
    per_seq = seq // tm
    b0 = col0 // width
    rows = [pl.BlockSpec((tm, width), functools.partial(lambda s, i: (i, b0 + s), s)) for s in range(3)]
    halos = [pl.BlockSpec((HALO, width),
                          functools.partial(lambda s, i: (jnp.maximum(i * (tm // HALO) - 1, 0), b0 + s), s))
             for s in range(3)]
    taps = [pl.BlockSpec((DN_CONV_TAPS, width), functools.partial(lambda s, i: (0, s), s)) for s in range(3)]
    out = pl.BlockSpec((tm, width), lambda i: (i, 0))
    shp = jax.ShapeDtypeStruct((m, width), BF16)
    return pl.pallas_call(
        functools.partial(_qkv_kernel, per_seq),
        grid=(m // tm,),
        in_specs=rows + halos + taps,
        out_specs=[out, out, out],
        out_shape=[shp, shp, shp],
        scratch_shapes=[pltpu.VMEM((tm + HALO, width), F32)],
        compiler_params=_params("parallel"),
        name="dn_qkv_conv",
    )(proj, proj, proj, proj, proj, proj, dn_conv, dn_conv, dn_conv)


def _unit_lower_inverses(lows):
    n = lows[0].shape[0]
    ri = lax.broadcasted_iota(jnp.int32, (n, n), 0)
    ci = lax.broadcasted_iota(jnp.int32, (n, n), 1)
    same = lambda bits: jnp.right_shift(ri, bits) == jnp.right_shift(ci, bits)
    eye = jnp.where(ri == ci, 1.0, 0.0)
    invs = [eye - jnp.where(same(1), low, 0.0) for low in lows]
    bits = 1
    while (1 << bits) < n:
        mask = same(bits + 1) & jnp.logical_not(same(bits))
        inv16 = [inv.astype(BF16) for inv in invs]
        right = [_dot(jnp.where(mask, low, 0.0).astype(BF16), i16) for low, i16 in zip(lows, inv16)]
        invs = [inv - _dot(i16, r.astype(BF16)) for inv, i16, r in zip(invs, inv16, right)]
        bits += 1
    return invs


def _delta_kernel(heads, q_ref, k_ref, v_ref, z_ref, g_ref, gt_ref, nw_ref, o_ref, state_ref):
    rows, width = q_ref.shape
    hp = width // HEAD_DIM
    c = CHUNK

    @pl.when(pl.program_id(2) == 0)
    def _():
        state_ref[...] = jnp.zeros_like(state_ref)

    gates = g_ref[...]
    lane = lax.broadcasted_iota(jnp.int32, gates.shape, 1)
    ri = lax.broadcasted_iota(jnp.int32, (c, c), 0)
    ci = lax.broadcasted_iota(jnp.int32, (c, c), 1)
    causal = ri >= ci
    strict = ri > ci

    work = []
    for hh in range(hp):
        head = pl.program_id(1) * hp + hh
        cols = slice(hh * HEAD_DIM, (hh + 1) * HEAD_DIM)
        beta_col = jnp.sum(jnp.where(lane == head, gates, 0.0), axis=-1, keepdims=True)
        gc_col = jnp.sum(jnp.where(lane == heads + head, gates, 0.0), axis=-1, keepdims=True)
        gc_row = gt_ref[pl.ds(heads + head, 1), :]
        for r0 in range(0, rows, c):
            rsl = slice(r0, r0 + c)
            q = q_ref[rsl, cols]
            k16 = k_ref[rsl, cols]
            k = k16.astype(F32)
            beta = beta_col[rsl]
            gcc = gc_col[rsl]
            gcr = gc_row[:, rsl]
            g_last = gcr[:, c - 1:c]
            decay = jnp.exp(jnp.where(causal, gcc - gcr, -jnp.inf))
            e_g = jnp.exp(gcc)
            kb = k * beta
            kq_k = _dot_nt(jnp.concatenate([kb.astype(BF16), q], axis=0), k16)
            low = jnp.where(strict, kq_k[:c] * decay, 0.0)
            qk = jnp.where(causal, kq_k[c:] * decay, 0.0)
            work.append(dict(
                hh=hh, rsl=rsl, cols=cols, low=low,
                vb_kbg16=jnp.concatenate([(v_ref[rsl, cols].astype(F32) * beta).astype(BF16),
                                          (kb * e_g).astype(BF16)], axis=1),
                qd16=(q.astype(F32) * e_g).astype(BF16),
                qk_kdt16=jnp.concatenate([qk.astype(BF16),
                                          (k * jnp.exp(g_last - gcc)).T.astype(BF16)], axis=0),
                g_end=jnp.exp(g_last)))
    for item, t_inv in zip(work, _unit_lower_inverses([item["low"] for item in work])):
        uw = _dot(t_inv.astype(BF16), item["vb_kbg16"])
        item["u"] = uw[:, :HEAD_DIM]
        item["w_qd16"] = jnp.concatenate([uw[:, HEAD_DIM:].astype(BF16), item["qd16"]], axis=0)

    for hh in range(hp):
        state = state_ref[hh]
        for item in work:
            if item["hh"] != hh:
                continue
            from_state = _dot(item["w_qd16"], state.astype(BF16))
            vn16 = (item["u"] - from_state[:c]).astype(BF16)
            from_new = _dot(item["qk_kdt16"], vn16)
            item["o"] = from_state[c:] + from_new[:c]
            state = state * item["g_end"] + from_new[c:]
        state_ref[hh] = state

    for item in work:
        rsl, cols = item["rsl"], item["cols"]
        z = z_ref[rsl, cols].astype(F32)
        o_ref[rsl, cols] = (_rms(item["o"]) * nw_ref[...] * _silu(z)).astype(o_ref.dtype)


def _delta(q, k, v, proj, z_col0, gates, gates_t, dn_norm, batch, seq, heads):
    m, width = q.shape
    hp = min(4, heads)
    rows = min(2 * CHUNK, seq)
    per_seq = seq // rows
    bw = hp * HEAD_DIM
    zb0 = z_col0 // bw
    qkv = pl.BlockSpec((rows, bw), lambda b, h, t: (b * per_seq + t, h))
    return pl.pallas_call(
        functools.partial(_delta_kernel, heads),
        grid=(batch, heads // hp, per_seq),
        in_specs=[qkv, qkv, qkv,
                  pl.BlockSpec((rows, bw), lambda b, h, t: (b * per_seq + t, zb0 + h)),
                  pl.BlockSpec((rows, LANES), lambda b, h, t: (b * per_seq + t, 0)),
                  pl.BlockSpec((2 * heads, rows), lambda b, h, t: (0, b * per_seq + t)),
                  pl.BlockSpec((1, HEAD_DIM), lambda b, h, t: (0, 0))],
        out_specs=qkv,
        out_shape=jax.ShapeDtypeStruct((m, width), BF16),
        scratch_shapes=[pltpu.VMEM((hp, HEAD_DIM, HEAD_DIM), F32)],
        compiler_params=_params("parallel", "parallel", "arbitrary"),
        name="gated_delta",
    )(q, k, v, proj, gates, gates_t, dn_norm.reshape(1, HEAD_DIM))


def _merge_kernel(off, ga_ref, gax_ref, gb_ref, gbx_ref, tail_ref, pa_ref, ob_ref, pp_ref, dp_ref, y_ref):
    tn = y_ref.shape[1]
    last = pl.program_id(0) == pl.num_programs(0) - 1

    def gate(main, extra):
        window = jnp.concatenate([main.astype(F32), extra.astype(F32)], axis=1)
        return _sigmoid(window[:, off:off + tn])

    gb_extra = jnp.where(last, tail_ref[...], gbx_ref[...])
    ya = gate(ga_ref[...], gax_ref[...]) * _dot(pa_ref[...], pp_ref[...].astype(BF16))
    yb = gate(gb_ref[...], gb_extra) * _dot(ob_ref[...], dp_ref[...].astype(BF16))
    y_ref[...] = (ya + yb).astype(y_ref.dtype)


def _merge(proj, tail, c_gate, pa, ob, pool_proj, dn_proj, layer):
    m, kp = pa.shape
    d = pool_proj.shape[2]
    tm, tn = min(1024, m), min(512, d)
    nb = d // tn
    off = c_gate % LANES
    a0, b0 = c_gate - off, c_gate - off + d
    assert a0 % tn == 0 and b0 + d == proj.shape[1] and tn % LANES == 0
    lanes_per_tile = tn // LANES
    last_lane_block = proj.shape[1] // LANES - 1
    return pl.pallas_call(
        functools.partial(_merge_kernel, off),
        grid=(nb, m // tm),
        in_specs=[pl.BlockSpec((tm, tn), lambda j, i: (i, a0 // tn + j)),
                  pl.BlockSpec((tm, LANES), lambda j, i: (i, a0 // LANES + lanes_per_tile * (j + 1))),
                  pl.BlockSpec((tm, tn), lambda j, i: (i, b0 // tn + j)),
                  pl.BlockSpec((tm, LANES), lambda j, i: (
                      i, jnp.minimum(b0 // LANES + lanes_per_tile * (j + 1), last_lane_block))),
                  pl.BlockSpec((tm, LANES), lambda j, i: (i, 0)),
                  pl.BlockSpec((tm, kp), lambda j, i: (i, 0)),
                  pl.BlockSpec((tm, kp), lambda j, i: (i, 0)),
                  pl.BlockSpec((None, kp, tn), lambda j, i: (layer, 0, j)),
                  pl.BlockSpec((None, kp, tn), lambda j, i: (layer, 0, j))],
        out_specs=pl.BlockSpec((tm, tn), lambda j, i: (i, j)),
        out_shape=jax.ShapeDtypeStruct((m, d), BF16),
        compiler_params=_params("parallel", "arbitrary"),
        name="gated_merge",
    )(proj, proj, proj, proj, tail, pa, ob, pool_proj, dn_proj)


FFN_CARRY = 8


def _ffn_up_kernel(per_seq, h_ref, wa_ref, wu0_ref, wu1_ref, cw_ref, cb_ref, o_ref, a_ref):
    tm = h_ref.shape[0]

    @pl.when(pl.program_id(1) % per_seq == 0)
    def _():
        a_ref[0:FFN_CARRY, :] = jnp.zeros((FFN_CARRY, a_ref.shape[1]), F32)

    h = h_ref[...]
    a_ref[FFN_CARRY:, :] = _dot(h, wa_ref[...])
    u = jnp.concatenate([_dot(h, wu0_ref[...]), _dot(h, wu1_ref[...])], axis=1)
    a = cb_ref[...]
    for j in range(FFN_CONV_TAPS):
        r0 = FFN_CARRY - (FFN_CONV_TAPS - 1) + j
        a = a + a_ref[r0:r0 + tm, :] * cw_ref[j:j + 1, :]
    o_ref[...] = (_silu(a) * u).astype(o_ref.dtype)
    a_ref[0:FFN_CARRY, :] = a_ref[tm:tm + FFN_CARRY, :]


def _ffn_up(h, w_up, layer, conv_w, conv_b, seq):
    m, d = h.shape
    n = w_up.shape[2] // 2
    tm, tn = min(1024, seq), 512
    half = tn // 2
    assert n % half == 0
    nh = n // half
    nt = pl.cdiv(n, tn)
    per_seq = seq // tm
    pad = nt * tn - n
    conv_w = jnp.pad(conv_w, ((0, 0), (0, 0), (0, pad)))
    conv_b = jnp.pad(conv_b, ((0, 0), (0, pad))).reshape(conv_b.shape[0], 1, nt * tn)
    return pl.pallas_call(
        functools.partial(_ffn_up_kernel, per_seq),
        grid=(nt, m // tm),
        in_specs=[pl.BlockSpec((tm, d), lambda j, i: (i, 0)),
                  pl.BlockSpec((None, d, tn), lambda j, i: (layer, 0, j)),
                  pl.BlockSpec((None, d, half), lambda j, i: (layer, 0, nh + 2 * j)),
                  pl.BlockSpec((None, d, half), lambda j, i: (layer, 0, jnp.minimum(nh + 2 * j + 1, 2 * nh - 1))),
                  pl.BlockSpec((None, FFN_CONV_TAPS, tn), lambda j, i: (layer, 0, j)),
                  pl.BlockSpec((None, 1, tn), lambda j, i: (layer, 0, j))],
        out_specs=pl.BlockSpec((tm, tn), lambda j, i: (i, j)),
        out_shape=jax.ShapeDtypeStruct((m, n), BF16),
        scratch_shapes=[pltpu.VMEM((tm + FFN_CARRY, tn), F32)],
        compiler_params=_params("arbitrary", "arbitrary"),
        name="ffn_up_conv_gate",
    )(h, w_up, w_up, w_up, conv_w, conv_b)


def kernel(x, c, w_ada, b_ada, ada_table, mix_pre_norm, w_in, pool_w, pool_scale, pool_proj,
           dn_conv, dn_a_log, dn_dt_bias, dn_norm, dn_proj, w_o, mix_post_norm,
           ffn_pre_norm, w_up, ffn_conv, ffn_conv_b, w_down, ffn_post_norm):
    batch, seq, d = x.shape
    depth = w_in.shape[0]
    m = batch * seq
    pool_width = pool_scale.shape[1]
    dn_width = dn_proj.shape[1]
    heads = dn_a_log.shape[1]
    c_qkv = pool_width
    c_z = c_qkv + 3 * dn_width
    c_ba = c_z + dn_width
    c_gate = c_ba + 2 * heads

    c8 = jnp.pad(c, ((0, 8 - batch), (0, 0)))
    mod = _ada(c8, w_ada, b_ada, ada_table.reshape(depth, N_MOD * d))
    mod = mod[:, :batch].reshape(depth, batch, N_MOD, 1, d)

    xf = x.reshape(m, d)
    h = _norm_mod(xf, mix_pre_norm[0], mod[0, :, 1], mod[0, :, 0], seq)
    w_up16, w_down16 = w_up.astype(BF16), w_down.astype(BF16)
    c_total = w_in.shape[2]
    n_main = c_total - c_gate % LANES
    w_tail = jnp.pad(w_in[:, :, n_main:], ((0, 0), (0, 0), (0, LANES - (c_total - n_main))))
    for l in range(depth):
        shift_f, scale_f, gate_m, gate_f = mod[l, :, 3], mod[l, :, 4], mod[l, :, 2], mod[l, :, 5]
        proj = _matmul(h, w_in, l, n_main, BF16, 1024, 512, "in_proj")
        tail = _matmul(h, w_tail, l, LANES, BF16, 1024, LANES, "in_proj_tail")
        gates = _ba(h, w_in, l, c_ba, dn_a_log[l], dn_dt_bias[l], heads)
        gates_t = gates[:, :2 * heads].T
        pa = _pool(proj, pool_w, l, pool_scale[l], seq)
        q, k, v = _qkv_conv(proj, dn_conv[l], dn_width, c_qkv, seq)
        ob = _delta(q, k, v, proj, c_z, gates, gates_t, dn_norm[l], batch, seq, heads)
        y = _merge(proj, tail, c_gate, pa, ob, pool_proj, dn_proj, l)
        yo = _matmul(y, w_o, l, d, BF16, 1024, 512, "out_proj")
        xf, h = _resid_norm(yo, xf, mix_post_norm[l], gate_m, seq,
                            nxt=(ffn_pre_norm[l], scale_f, shift_f))
        act = _ffn_up(h, w_up16, l, ffn_conv, ffn_conv_b, seq)
        yd = _matmul(act, w_down16, l, d, BF16, 512, 512, "ffn_down")
        if l + 1 < depth:
            xf, h = _resid_norm(yd, xf, ffn_post_norm[l], gate_f, seq,
                                nxt=(mix_pre_norm[l + 1], mod[l + 1, :, 1], mod[l + 1, :, 0]))
        else:
            xf = _resid_norm(yd, xf, ffn_post_norm[l], gate_f, seq)
    return xf.reshape(batch, seq, d)
```

```python
import functools

import jax
import jax.numpy as jnp
from jax import lax
from jax.experimental import pallas as pl
from jax.experimental.pallas import tpu as pltpu

F32 = jnp.float32
BF16 = jnp.bfloat16

EPS = 1e-6
POOL_WINDOWS = (2, 4, 8, 16)
HEAD_DIM = 128
DN_CONV_TAPS = 4
FFN_CONV_TAPS = 3
N_MOD = 6
CHUNK = 128
HALO = 16
VMEM_LIMIT_BYTES = 56 * 1024 * 1024
LANES = 128


def _params(*semantics):
    return pltpu.CompilerParams(dimension_semantics=semantics,
                                vmem_limit_bytes=VMEM_LIMIT_BYTES)


def _dot(a, b):
    return jnp.dot(a, b, preferred_element_type=F32)


def _dot_nt(a, b):
    return lax.dot_general(a, b, (((1,), (1,)), ((), ())), preferred_element_type=F32)


def _sigmoid(x):
    return 1.0 / (1.0 + jnp.exp(-x))


def _silu(x):
    return x * _sigmoid(x)


def _softplus(x):
    return jnp.maximum(x, 0.0) + jnp.log1p(jnp.exp(-jnp.abs(x)))


def _rms(x):
    return x * lax.rsqrt(jnp.mean(x * x, axis=-1, keepdims=True) + EPS)


def _ada_kernel(c_ref, w_ref, b_ref, t_ref, o_ref):
    a = _silu(c_ref[...]).astype(BF16)
    acc = _dot(a, w_ref[...].astype(BF16)) + b_ref[...]
    o_ref[...] = acc[None] + t_ref[...]


def _ada(c8, w_ada, b_ada, table):
    depth = table.shape[0]
    d, n = w_ada.shape
    tn = min(512, n)
    return pl.pallas_call(
        _ada_kernel,
        grid=(n // tn,),
        in_specs=[pl.BlockSpec((8, d), lambda j: (0, 0)),
                  pl.BlockSpec((d, tn), lambda j: (0, j)),
                  pl.BlockSpec((1, tn), lambda j: (0, j)),
                  pl.BlockSpec((depth, 1, tn), lambda j: (0, 0, j))],
        out_specs=pl.BlockSpec((depth, 8, tn), lambda j: (0, 0, j)),
        out_shape=jax.ShapeDtypeStruct((depth, 8, n), F32),
        compiler_params=_params("arbitrary"),
        name="ada",
    )(c8, w_ada, b_ada.reshape(1, n), table.reshape(depth, 1, n))


def _norm_mod_kernel(x_ref, w_ref, sc_ref, sh_ref, h_ref):
    y = _rms(x_ref[...]) * w_ref[...]
    h_ref[...] = (y * (1.0 + sc_ref[0]) + sh_ref[0]).astype(h_ref.dtype)


def _norm_mod(x, w, scale, shift, seq):
    m, d = x.shape
    tm = min(256, seq)
    per_seq = seq // tm
    row = pl.BlockSpec((tm, d), lambda i: (i, 0))
    vec = pl.BlockSpec((1, d), lambda i: (0, 0))
    bvec = pl.BlockSpec((1, 1, d), lambda i: (i // per_seq, 0, 0))
    return pl.pallas_call(
        _norm_mod_kernel,
        grid=(m // tm,),
        in_specs=[row, vec, bvec, bvec],
        out_specs=row,
        out_shape=jax.ShapeDtypeStruct((m, d), BF16),
        compiler_params=_params("parallel"),
        name="norm_mod",
    )(x, w.reshape(1, d), scale, shift)


def _resid_norm_kernel(y_ref, x_ref, pw_ref, g_ref, nw_ref, sc_ref, sh_ref, xo_ref, h_ref):
    xn = x_ref[...] + g_ref[0] * (_rms(y_ref[...].astype(F32)) * pw_ref[...])
    xo_ref[...] = xn
    h = _rms(xn) * nw_ref[...]
    h_ref[...] = (h * (1.0 + sc_ref[0]) + sh_ref[0]).astype(h_ref.dtype)


def _resid_kernel(y_ref, x_ref, pw_ref, g_ref, xo_ref):
    xo_ref[...] = x_ref[...] + g_ref[0] * (_rms(y_ref[...].astype(F32)) * pw_ref[...])


def _resid_norm(y, x, post_w, gate, seq, nxt=None):
    m, d = x.shape
    tm = min(256, seq)
    per_seq = seq // tm
    row = pl.BlockSpec((tm, d), lambda i: (i, 0))
    vec = pl.BlockSpec((1, d), lambda i: (0, 0))
    bvec = pl.BlockSpec((1, 1, d), lambda i: (i // per_seq, 0, 0))
    if nxt is None:
        return pl.pallas_call(
            _resid_kernel,
            grid=(m // tm,),
            in_specs=[row, row, vec, bvec],
            out_specs=row,
            out_shape=jax.ShapeDtypeStruct((m, d), F32),
            compiler_params=_params("parallel"),
            name="resid",
        )(y, x, post_w.reshape(1, d), gate)
    norm_w, scale, shift = nxt
    return pl.pallas_call(
        _resid_norm_kernel,
        grid=(m // tm,),
        in_specs=[row, row, vec, bvec, vec, bvec, bvec],
        out_specs=[row, row],
        out_shape=[jax.ShapeDtypeStruct((m, d), F32), jax.ShapeDtypeStruct((m, d), BF16)],
        compiler_params=_params("parallel"),
        name="resid_norm",
    )(y, x, post_w.reshape(1, d), gate, norm_w.reshape(1, d), scale, shift)


def _mm_kernel(a_ref, w_ref, o_ref):
    o_ref[...] = _dot(a_ref[...], w_ref[...].astype(BF16)).astype(o_ref.dtype)


def _matmul(a, w, layer, n, out_dtype, tm, tn, name):
    m, k = a.shape
    tm, tn = min(tm, m), min(tn, n)
    return pl.pallas_call(
        _mm_kernel,
        grid=(n // tn, m // tm),
        in_specs=[pl.BlockSpec((tm, k), lambda j, i: (i, 0)),
                  pl.BlockSpec((None, k, tn), lambda j, i: (layer, 0, j))],
        out_specs=pl.BlockSpec((tm, tn), lambda j, i: (i, j)),
        out_shape=jax.ShapeDtypeStruct((m, n), out_dtype),
        compiler_params=_params("parallel", "arbitrary"),
        name=name,
    )(a, w)


def _mm_nt_kernel(a_ref, wt_ref, o_ref):
    o_ref[...] = _dot_nt(a_ref[...], wt_ref[0].astype(BF16)).astype(o_ref.dtype)


def _matmul_nt(a, wt, layer, row0, n, out_dtype, tm, tn, name):
    m, k = a.shape
    tm, tn = min(tm, m), min(tn, n)
    sublanes = 8 * 4 // wt.dtype.itemsize
    assert row0 % sublanes == 0 and n % tn == 0
    return pl.pallas_call(
        _mm_nt_kernel,
        grid=(n // tn, m // tm),
        in_specs=[pl.BlockSpec((tm, k), lambda j, i: (i, 0)),
                  pl.BlockSpec((pl.Element(1), pl.Element(tn), pl.Element(k)),
                               lambda j, i: (layer, pl.multiple_of(row0 + j * tn, sublanes), 0))],
        out_specs=pl.BlockSpec((tm, tn), lambda j, i: (i, j)),
        out_shape=jax.ShapeDtypeStruct((m, n), out_dtype),
        compiler_params=_params("parallel", "arbitrary"),
        name=name,
    )(a, wt)


def _ba_kernel(heads, h_ref, w_ref, alog_ref, dtb_ref, o_ref):
    p = _dot_nt(h_ref[...], w_ref[...].astype(BF16))
    lane = lax.broadcasted_iota(jnp.int32, p.shape, 1)
    is_g = (lane >= heads) & (lane < 2 * heads)
    g = jnp.where(is_g, -jnp.exp(alog_ref[...]) * _softplus(p + dtb_ref[...]), 0.0)
    pos = lax.broadcasted_iota(jnp.int32, p.shape, 0) % CHUNK
    shift = 1
    while shift < CHUNK:
        g = g + jnp.where(pos >= shift, pltpu.roll(g, shift, axis=0), 0.0)
        shift *= 2
    o_ref[...] = jnp.where(is_g, g, _sigmoid(p))


def _ba(h, wt, layer, row0, a_log, dt_bias, heads):
    m, k = h.shape
    tm = min(512, m)
    assert row0 % LANES == 0 and 2 * heads <= LANES
    pad = lambda v: jnp.zeros((1, LANES), F32).at[0, heads:2 * heads].set(v)
    return pl.pallas_call(
        functools.partial(_ba_kernel, heads),
        grid=(m // tm,),
        in_specs=[pl.BlockSpec((tm, k), lambda i: (i, 0)),
                  pl.BlockSpec((None, LANES, k), lambda i: (layer, row0 // LANES, 0)),
                  pl.BlockSpec((1, LANES), lambda i: (0, 0)),
                  pl.BlockSpec((1, LANES), lambda i: (0, 0))],
        out_specs=pl.BlockSpec((tm, LANES), lambda i: (i, 0)),
        out_shape=jax.ShapeDtypeStruct((m, LANES), F32),
        compiler_params=_params("parallel"),
        name="dn_gates",
    )(h, wt, pad(a_log), pad(dt_bias))


def _pool_kernel(per_seq, u_ref, halo_ref, pw_ref, sc_ref, o_ref, buf_ref, pa_ref):
    tm, width = u_ref.shape
    group = width // len(POOL_WINDOWS)
    t = pl.program_id(0) % per_seq
    halo = halo_ref[...].astype(F32)
    buf_ref[0:HALO, :] = jnp.where(t == 0, 0.0, halo)
    buf_ref[HALO:, :] = u_ref[...].astype(F32)
    pos = t * tm + lax.broadcasted_iota(jnp.int32, (tm, LANES), 0)
    for gi, win in enumerate(POOL_WINDOWS):
        inv_count = 1.0 / jnp.minimum(pos + 1, win).astype(F32)
        for c in range(gi * group, (gi + 1) * group, LANES):
            tok = buf_ref[HALO:HALO + tm, c:c + LANES]
            acc = tok
            for j in range(1, win):
                acc = acc + buf_ref[HALO - j:HALO - j + tm, c:c + LANES]
            pa_ref[:, c:c + LANES] = (acc * inv_count - tok).astype(BF16)
    for gi in range(len(POOL_WINDOWS)):
        cols = slice(gi * group, (gi + 1) * group)
        r = _dot(pa_ref[:, cols], pw_ref[gi].astype(BF16)) * sc_ref[:, cols]
        o_ref[:, cols] = r.astype(o_ref.dtype)


def _pool(proj, pool_w, layer, pool_scale, seq):
    m = proj.shape[0]
    _, groups, group, _ = pool_w.shape
    width = groups * group
    tm = min(256, seq)
    per_seq = seq // tm
    return pl.pallas_call(
        functools.partial(_pool_kernel, per_seq),
        grid=(m // tm,),
        in_specs=[pl.BlockSpec((tm, width), lambda i: (i, 0)),
                  pl.BlockSpec((HALO, width), lambda i: (jnp.maximum(i * (tm // HALO) - 1, 0), 0)),
                  pl.BlockSpec((None, groups, group, group), lambda i: (layer, 0, 0, 0)),
                  pl.BlockSpec((1, width), lambda i: (0, 0))],
        out_specs=pl.BlockSpec((tm, width), lambda i: (i, 0)),
        out_shape=jax.ShapeDtypeStruct((m, width), BF16),
        scratch_shapes=[pltpu.VMEM((tm + HALO, width), F32), pltpu.VMEM((tm, width), BF16)],
        compiler_params=_params("parallel"),
        name="pool_mixer",
    )(proj, proj, pool_w, pool_scale.reshape(1, width))


def _qkv_kernel(per_seq, q_ref, k_ref, v_ref, hq_ref, hk_ref, hv_ref, wq_ref, wk_ref, wv_ref,
                qo_ref, ko_ref, vo_ref, buf_ref):
    tm, width = q_ref.shape
    t = pl.program_id(0) % per_seq
    first = t == 0
    for x_ref, halo_ref, w_ref, o_ref, norm in (
            (q_ref, hq_ref, wq_ref, qo_ref, HEAD_DIM ** -0.5),
            (k_ref, hk_ref, wk_ref, ko_ref, 1.0),
            (v_ref, hv_ref, wv_ref, vo_ref, None)):
        buf_ref[0:HALO, :] = jnp.where(first, 0.0, halo_ref[...].astype(F32))
        buf_ref[HALO:, :] = x_ref[...].astype(F32)
        for c in range(0, width, HEAD_DIM):
            cols = slice(c, c + HEAD_DIM)
            y = None
            for j in range(DN_CONV_TAPS):
                r0 = HALO - (DN_CONV_TAPS - 1) + j
                term = buf_ref[r0:r0 + tm, cols] * w_ref[j:j + 1, cols]
                y = term if y is None else y + term
            y = _silu(y)
            if norm is not None:
                y = y * lax.rsqrt(jnp.sum(y * y, axis=-1, keepdims=True) + EPS) * norm
            o_ref[:, cols] = y.astype(o_ref.dtype)


def _qkv_conv(proj, dn_conv, width, col0, seq):
    m = proj.shape[0]
    tm = min(128, seq)
    per_seq = seq // tm
    b0 = col0 // width
    rows = [pl.BlockSpec((tm, width), functools.partial(lambda s, i: (i, b0 + s), s)) for s in range(3)]
    halos = [pl.BlockSpec((HALO, width),
                          functools.partial(lambda s, i: (jnp.maximum(i * (tm // HALO) - 1, 0), b0 + s), s))
             for s in range(3)]
    taps = [pl.BlockSpec((DN_CONV_TAPS, width), functools.partial(lambda s, i: (0, s), s)) for s in range(3)]
    out = pl.BlockSpec((tm, width), lambda i: (i, 0))
    shp = jax.ShapeDtypeStruct((m, width), BF16)
    return pl.pallas_call(
        functools.partial(_qkv_kernel, per_seq),
        grid=(m // tm,),
        in_specs=rows + halos + taps,
        out_specs=[out, out, out],
        out_shape=[shp, shp, shp],
        scratch_shapes=[pltpu.VMEM((tm + HALO, width), F32)],
        compiler_params=_params("parallel"),
        name="dn_qkv_conv",
    )(proj, proj, proj, proj, proj, proj, dn_conv, dn_conv, dn_conv)


def _unit_lower_inverses(lows):
    n = lows[0].shape[0]
    ri = lax.broadcasted_iota(jnp.int32, (n, n), 0)
    ci = lax.broadcasted_iota(jnp.int32, (n, n), 1)
    same = lambda bits: jnp.right_shift(ri, bits) == jnp.right_shift(ci, bits)
    eye = jnp.where(ri == ci, 1.0, 0.0)
    invs = [eye - jnp.where(same(1), low, 0.0) for low in lows]
    bits = 1
    while (1 << bits) < n:
        mask = same(bits + 1) & jnp.logical_not(same(bits))
        inv16 = [inv.astype(BF16) for inv in invs]
        right = [_dot(jnp.where(mask, low, 0.0).astype(BF16), i16) for low, i16 in zip(lows, inv16)]
        invs = [inv - _dot(i16, r.astype(BF16)) for inv, i16, r in zip(invs, inv16, right)]
        bits += 1
    return invs


def _delta_kernel(heads, q_ref, k_ref, v_ref, z_ref, g_ref, gt_ref, nw_ref, o_ref, state_ref):
    rows, width = q_ref.shape
    hp = width // HEAD_DIM
    c = CHUNK

    @pl.when(pl.program_id(2) == 0)
    def _():
        state_ref[...] = jnp.zeros_like(state_ref)

    gates = g_ref[...]
    lane = lax.broadcasted_iota(jnp.int32, gates.shape, 1)
    ri = lax.broadcasted_iota(jnp.int32, (c, c), 0)
    ci = lax.broadcasted_iota(jnp.int32, (c, c), 1)
    causal = ri >= ci
    strict = ri > ci

    work = []
    for hh in range(hp):
        head = pl.program_id(1) * hp + hh
        cols = slice(hh * HEAD_DIM, (hh + 1) * HEAD_DIM)
        beta_col = jnp.sum(jnp.where(lane == head, gates, 0.0), axis=-1, keepdims=True)
        gc_col = jnp.sum(jnp.where(lane == heads + head, gates, 0.0), axis=-1, keepdims=True)
        gc_row = gt_ref[pl.ds(heads + head, 1), :]
        for r0 in range(0, rows, c):
            rsl = slice(r0, r0 + c)
            q = q_ref[rsl, cols]
            k16 = k_ref[rsl, cols]
            k = k16.astype(F32)
            beta = beta_col[rsl]
            gcc = gc_col[rsl]
            gcr = gc_row[:, rsl]
            g_last = gcr[:, c - 1:c]
            decay = jnp.exp(jnp.where(causal, gcc - gcr, -jnp.inf))
            e_g = jnp.exp(gcc)
            kb = k * beta
            kq_k = _dot_nt(jnp.concatenate([kb.astype(BF16), q], axis=0), k16)
            low = jnp.where(strict, kq_k[:c] * decay, 0.0)
            qk = jnp.where(causal, kq_k[c:] * decay, 0.0)
            work.append(dict(
                hh=hh, rsl=rsl, cols=cols, low=low,
                vb_kbg16=jnp.concatenate([(v_ref[rsl, cols].astype(F32) * beta).astype(BF16),
                                          (kb * e_g).astype(BF16)], axis=1),
                qd16=(q.astype(F32) * e_g).astype(BF16),
                qk_kdt16=jnp.concatenate([qk.astype(BF16),
                                          (k * jnp.exp(g_last - gcc)).T.astype(BF16)], axis=0),
                g_end=jnp.exp(g_last)))
    for item, t_inv in zip(work, _unit_lower_inverses([item["low"] for item in work])):
        uw = _dot(t_inv.astype(BF16), item["vb_kbg16"])
        item["u"] = uw[:, :HEAD_DIM]
        item["w_qd16"] = jnp.concatenate([uw[:, HEAD_DIM:].astype(BF16), item["qd16"]], axis=0)

    for hh in range(hp):
        state = state_ref[hh]
        for item in work:
            if item["hh"] != hh:
                continue
            from_state = _dot(item["w_qd16"], state.astype(BF16))
            vn16 = (item["u"] - from_state[:c]).astype(BF16)
            from_new = _dot(item["qk_kdt16"], vn16)
            item["o"] = from_state[c:] + from_new[:c]
            state = state * item["g_end"] + from_new[c:]
        state_ref[hh] = state

    for item in work:
        rsl, cols = item["rsl"], item["cols"]
        z = z_ref[rsl, cols].astype(F32)
        o_ref[rsl, cols] = (_rms(item["o"]) * nw_ref[...] * _silu(z)).astype(o_ref.dtype)


def _delta(q, k, v, proj, z_col0, gates, gates_t, dn_norm, batch, seq, heads):
    m, width = q.shape
    hp = min(4, heads)
    rows = min(2 * CHUNK, seq)
    per_seq = seq // rows
    bw = hp * HEAD_DIM
    zb0 = z_col0 // bw
    qkv = pl.BlockSpec((rows, bw), lambda b, h, t: (b * per_seq + t, h))
    return pl.pallas_call(
        functools.partial(_delta_kernel, heads),
        grid=(batch, heads // hp, per_seq),
        in_specs=[qkv, qkv, qkv,
                  pl.BlockSpec((rows, bw), lambda b, h, t: (b * per_seq + t, zb0 + h)),
                  pl.BlockSpec((rows, LANES), lambda b, h, t: (b * per_seq + t, 0)),
                  pl.BlockSpec((2 * heads, rows), lambda b, h, t: (0, b * per_seq + t)),
                  pl.BlockSpec((1, HEAD_DIM), lambda b, h, t: (0, 0))],
        out_specs=qkv,
        out_shape=jax.ShapeDtypeStruct((m, width), BF16),
        scratch_shapes=[pltpu.VMEM((hp, HEAD_DIM, HEAD_DIM), F32)],
        compiler_params=_params("parallel", "parallel", "arbitrary"),
        name="gated_delta",
    )(q, k, v, proj, gates, gates_t, dn_norm.reshape(1, HEAD_DIM))


def _merge_kernel(ga_ref, gb_ref, pa_ref, ob_ref, pp_ref, dp_ref, y_ref):
    ya = _sigmoid(ga_ref[...].astype(F32)) * _dot(pa_ref[...], pp_ref[...].astype(BF16))
    yb = _sigmoid(gb_ref[...].astype(F32)) * _dot(ob_ref[...], dp_ref[...].astype(BF16))
    y_ref[...] = (ya + yb).astype(y_ref.dtype)


def _merge(gate_logits, pa, ob, pool_proj, dn_proj, layer):
    m, kp = pa.shape
    d = pool_proj.shape[2]
    tm, tn = min(1024, m), min(512, d)
    nb = d // tn
    return pl.pallas_call(
        _merge_kernel,
        grid=(nb, m // tm),
        in_specs=[pl.BlockSpec((tm, tn), lambda j, i: (i, j)),
                  pl.BlockSpec((tm, tn), lambda j, i: (i, nb + j)),
                  pl.BlockSpec((tm, kp), lambda j, i: (i, 0)),
                  pl.BlockSpec((tm, kp), lambda j, i: (i, 0)),
                  pl.BlockSpec((None, kp, tn), lambda j, i: (layer, 0, j)),
                  pl.BlockSpec((None, kp, tn), lambda j, i: (layer, 0, j))],
        out_specs=pl.BlockSpec((tm, tn), lambda j, i: (i, j)),
        out_shape=jax.ShapeDtypeStruct((m, d), BF16),
        compiler_params=_params("parallel", "arbitrary"),
        name="gated_merge",
    )(gate_logits, gate_logits, pa, ob, pool_proj, dn_proj)


FFN_CARRY = 8


def _ffn_up_kernel(per_seq, h_ref, wa_ref, wu0_ref, wu1_ref, cw_ref, cb_ref, o_ref, a_ref):
    tm = h_ref.shape[0]

    @pl.when(pl.program_id(1) % per_seq == 0)
    def _():
        a_ref[0:FFN_CARRY, :] = jnp.zeros((FFN_CARRY, a_ref.shape[1]), F32)

    h = h_ref[...]
    a_ref[FFN_CARRY:, :] = _dot(h, wa_ref[...])
    u = jnp.concatenate([_dot(h, wu0_ref[...]), _dot(h, wu1_ref[...])], axis=1)
    a = cb_ref[...]
    for j in range(FFN_CONV_TAPS):
        r0 = FFN_CARRY - (FFN_CONV_TAPS - 1) + j
        a = a + a_ref[r0:r0 + tm, :] * cw_ref[j:j + 1, :]
    o_ref[...] = (_silu(a) * u).astype(o_ref.dtype)
    a_ref[0:FFN_CARRY, :] = a_ref[tm:tm + FFN_CARRY, :]


def _ffn_up(h, w_up, layer, conv_w, conv_b, seq):
    m, d = h.shape
    n = w_up.shape[2] // 2
    tm, tn = min(1024, seq), 512
    half = tn // 2
    assert n % half == 0
    nh = n // half
    nt = pl.cdiv(n, tn)
    per_seq = seq // tm
    pad = nt * tn - n
    conv_w = jnp.pad(conv_w, ((0, 0), (0, 0), (0, pad)))
    conv_b = jnp.pad(conv_b, ((0, 0), (0, pad))).reshape(conv_b.shape[0], 1, nt * tn)
    return pl.pallas_call(
        functools.partial(_ffn_up_kernel, per_seq),
        grid=(nt, m // tm),
        in_specs=[pl.BlockSpec((tm, d), lambda j, i: (i, 0)),
                  pl.BlockSpec((None, d, tn), lambda j, i: (layer, 0, j)),
                  pl.BlockSpec((None, d, half), lambda j, i: (layer, 0, nh + 2 * j)),
                  pl.BlockSpec((None, d, half), lambda j, i: (layer, 0, jnp.minimum(nh + 2 * j + 1, 2 * nh - 1))),
                  pl.BlockSpec((None, FFN_CONV_TAPS, tn), lambda j, i: (layer, 0, j)),
                  pl.BlockSpec((None, 1, tn), lambda j, i: (layer, 0, j))],
        out_specs=pl.BlockSpec((tm, tn), lambda j, i: (i, j)),
        out_shape=jax.ShapeDtypeStruct((m, n), BF16),
        scratch_shapes=[pltpu.VMEM((tm + FFN_CARRY, tn), F32)],
        compiler_params=_params("arbitrary", "arbitrary"),
        name="ffn_up_conv_gate",
    )(h, w_up, w_up, w_up, conv_w, conv_b)


def kernel(x, c, w_ada, b_ada, ada_table, mix_pre_norm, w_in, pool_w, pool_scale, pool_proj,
           dn_conv, dn_a_log, dn_dt_bias, dn_norm, dn_proj, w_o, mix_post_norm,
           ffn_pre_norm, w_up, ffn_conv, ffn_conv_b, w_down, ffn_post_norm):
    batch, seq, d = x.shape
    depth = w_in.shape[0]
    m = batch * seq
    pool_width = pool_scale.shape[1]
    dn_width = dn_proj.shape[1]
    heads = dn_a_log.shape[1]
    c_qkv = pool_width
    c_z = c_qkv + 3 * dn_width
    c_ba = c_z + dn_width
    c_gate = c_ba + 2 * heads

    c8 = jnp.pad(c, ((0, 8 - batch), (0, 0)))
    mod = _ada(c8, w_ada, b_ada, ada_table.reshape(depth, N_MOD * d))
    mod = mod[:, :batch].reshape(depth, batch, N_MOD, 1, d)

    xf = x.reshape(m, d)
    h = _norm_mod(xf, mix_pre_norm[0], mod[0, :, 1], mod[0, :, 0], seq)
    w_up16, w_down16 = w_up.astype(BF16), w_down.astype(BF16)
    w_in_t = jnp.swapaxes(w_in, 1, 2)
    for l in range(depth):
        shift_f, scale_f, gate_m, gate_f = mod[l, :, 3], mod[l, :, 4], mod[l, :, 2], mod[l, :, 5]
        proj = _matmul_nt(h, w_in_t, l, 0, c_ba, BF16, 1024, 512, "in_proj")
        gate_logits = _matmul_nt(h, w_in_t, l, c_gate, 2 * d, BF16, 1024, 512, "in_proj_gates")
        gates = _ba(h, w_in_t, l, c_ba, dn_a_log[l], dn_dt_bias[l], heads)
        gates_t = gates[:, :2 * heads].T
        pa = _pool(proj, pool_w, l, pool_scale[l], seq)
        q, k, v = _qkv_conv(proj, dn_conv[l], dn_width, c_qkv, seq)
        ob = _delta(q, k, v, proj, c_z, gates, gates_t, dn_norm[l], batch, seq, heads)
        y = _merge(gate_logits, pa, ob, pool_proj, dn_proj, l)
        yo = _matmul(y, w_o, l, d, BF16, 1024, 512, "out_proj")
        xf, h = _resid_norm(yo, xf, mix_post_norm[l], gate_m, seq,
                            nxt=(ffn_pre_norm[l], scale_f, shift_f))
        act = _ffn_up(h, w_up16, l, ffn_conv, ffn_conv_b, seq)
        yd = _matmul(act, w_down16, l, d, BF16, 512, 512, "ffn_down")
        if l + 1 < depth:
            xf, h = _resid_norm(yd, xf, ffn_post_norm[l], gate_f, seq,
                                nxt=(mix_pre_norm[l + 1], mod[l + 1, :, 1], mod[l + 1, :, 0]))
        else:
            xf = _resid_norm(yd, xf, ffn_post_norm[l], gate_f, seq)
    return xf.reshape(batch, seq, d)
```

```python
import functools

import jax
import jax.numpy as jnp
from jax import lax
from jax.experimental import pallas as pl
from jax.experimental.pallas import tpu as pltpu

F32 = jnp.float32
BF16 = jnp.bfloat16

EPS = 1e-6
POOL_WINDOWS = (2, 4, 8, 16)
HEAD_DIM = 128
DN_CONV_TAPS = 4
FFN_CONV_TAPS = 3
N_MOD = 6
CHUNK = 128
HALO = 16
VMEM_LIMIT_BYTES = 56 * 1024 * 1024
LANES = 128


def _params(*semantics):
    return pltpu.CompilerParams(dimension_semantics=semantics,
                                vmem_limit_bytes=VMEM_LIMIT_BYTES)


def _dot(a, b):
    return jnp.dot(a, b, preferred_element_type=F32)


def _dot_nt(a, b):
    return lax.dot_general(a, b, (((1,), (1,)), ((), ())), preferred_element_type=F32)


def _sigmoid(x):
    return 1.0 / (1.0 + jnp.exp(-x))


def _silu(x):
    return x * _sigmoid(x)


def _softplus(x):
    return jnp.maximum(x, 0.0) + jnp.log1p(jnp.exp(-jnp.abs(x)))


def _rms(x):
    return x * lax.rsqrt(jnp.mean(x * x, axis=-1, keepdims=True) + EPS)


def _ada_kernel(c_ref, w_ref, b_ref, t_ref, o_ref):
    a = _silu(c_ref[...]).astype(BF16)
    acc = _dot(a, w_ref[...].astype(BF16)) + b_ref[...]
    o_ref[...] = acc[None] + t_ref[...]


def _ada(c8, w_ada, b_ada, table):
    depth = table.shape[0]
    d, n = w_ada.shape
    tn = min(512, n)
    return pl.pallas_call(
        _ada_kernel,
        grid=(n // tn,),
        in_specs=[pl.BlockSpec((8, d), lambda j: (0, 0)),
                  pl.BlockSpec((d, tn), lambda j: (0, j)),
                  pl.BlockSpec((1, tn), lambda j: (0, j)),
                  pl.BlockSpec((depth, 1, tn), lambda j: (0, 0, j))],
        out_specs=pl.BlockSpec((depth, 8, tn), lambda j: (0, 0, j)),
        out_shape=jax.ShapeDtypeStruct((depth, 8, n), F32),
        compiler_params=_params("arbitrary"),
        name="ada",
    )(c8, w_ada, b_ada.reshape(1, n), table.reshape(depth, 1, n))


def _norm_mod_kernel(x_ref, w_ref, sc_ref, sh_ref, h_ref):
    y = _rms(x_ref[...]) * w_ref[...]
    h_ref[...] = (y * (1.0 + sc_ref[0]) + sh_ref[0]).astype(h_ref.dtype)


def _norm_mod(x, w, scale, shift, seq):
    m, d = x.shape
    tm = min(256, seq)
    per_seq = seq // tm
    row = pl.BlockSpec((tm, d), lambda i: (i, 0))
    vec = pl.BlockSpec((1, d), lambda i: (0, 0))
    bvec = pl.BlockSpec((1, 1, d), lambda i: (i // per_seq, 0, 0))
    return pl.pallas_call(
        _norm_mod_kernel,
        grid=(m // tm,),
        in_specs=[row, vec, bvec, bvec],
        out_specs=row,
        out_shape=jax.ShapeDtypeStruct((m, d), BF16),
        compiler_params=_params("parallel"),
        name="norm_mod",
    )(x, w.reshape(1, d), scale, shift)


def _resid_norm_kernel(y_ref, x_ref, pw_ref, g_ref, nw_ref, sc_ref, sh_ref, xo_ref, h_ref):
    xn = x_ref[...] + g_ref[0] * (_rms(y_ref[...].astype(F32)) * pw_ref[...])
    xo_ref[...] = xn
    h = _rms(xn) * nw_ref[...]
    h_ref[...] = (h * (1.0 + sc_ref[0]) + sh_ref[0]).astype(h_ref.dtype)


def _resid_kernel(y_ref, x_ref, pw_ref, g_ref, xo_ref):
    xo_ref[...] = x_ref[...] + g_ref[0] * (_rms(y_ref[...].astype(F32)) * pw_ref[...])


def _resid_norm(y, x, post_w, gate, seq, nxt=None):
    m, d = x.shape
    tm = min(256, seq)
    per_seq = seq // tm
    row = pl.BlockSpec((tm, d), lambda i: (i, 0))
    vec = pl.BlockSpec((1, d), lambda i: (0, 0))
    bvec = pl.BlockSpec((1, 1, d), lambda i: (i // per_seq, 0, 0))
    if nxt is None:
        return pl.pallas_call(
            _resid_kernel,
            grid=(m // tm,),
            in_specs=[row, row, vec, bvec],
            out_specs=row,
            out_shape=jax.ShapeDtypeStruct((m, d), F32),
            compiler_params=_params("parallel"),
            name="resid",
        )(y, x, post_w.reshape(1, d), gate)
    norm_w, scale, shift = nxt
    return pl.pallas_call(
        _resid_norm_kernel,
        grid=(m // tm,),
        in_specs=[row, row, vec, bvec, vec, bvec, bvec],
        out_specs=[row, row],
        out_shape=[jax.ShapeDtypeStruct((m, d), F32), jax.ShapeDtypeStruct((m, d), BF16)],
        compiler_params=_params("parallel"),
        name="resid_norm",
    )(y, x, post_w.reshape(1, d), gate, norm_w.reshape(1, d), scale, shift)


def _mm_kernel(a_ref, w_ref, o_ref):
    o_ref[...] = _dot(a_ref[...], w_ref[...].astype(BF16)).astype(o_ref.dtype)


def _matmul(a, w, layer, n, out_dtype, tm, tn, name, rows_outer=False):
    m, k = a.shape
    tm, tn = min(tm, m), min(tn, n)
    if rows_outer:
        grid, ij = (m // tm, n // tn), lambda i, j: (i, j)
    else:
        grid, ij = (n // tn, m // tm), lambda j, i: (i, j)
    return pl.pallas_call(
        _mm_kernel,
        grid=grid,
        in_specs=[pl.BlockSpec((tm, k), lambda *g: (ij(*g)[0], 0)),
                  pl.BlockSpec((None, k, tn), lambda *g: (layer, 0, ij(*g)[1]))],
        out_specs=pl.BlockSpec((tm, tn), lambda *g: ij(*g)),
        out_shape=jax.ShapeDtypeStruct((m, n), out_dtype),
        compiler_params=_params("parallel", "arbitrary"),
        name=name,
    )(a, w)


def _mm_nt_kernel(a_ref, wt_ref, o_ref):
    o_ref[...] = _dot_nt(a_ref[...], wt_ref[0].astype(BF16)).astype(o_ref.dtype)


def _matmul_nt(a, wt, layer, n, skip_at, skip, out_dtype, tm, tn, name):
    m, k = a.shape
    tm, tn = min(tm, m), min(tn, n)
    sublanes = 8 * 4 // wt.dtype.itemsize
    assert skip % sublanes == 0 and n % tn == 0 and skip_at % tn == 0
    first_after = skip_at // tn
    row = lambda j: pl.multiple_of(j * tn + jnp.where(j >= first_after, skip, 0), sublanes)
    return pl.pallas_call(
        _mm_nt_kernel,
        grid=(m // tm, n // tn),
        in_specs=[pl.BlockSpec((tm, k), lambda i, j: (i, 0)),
                  pl.BlockSpec((pl.Element(1), pl.Element(tn), pl.Element(k)),
                               lambda i, j: (layer, row(j), 0))],
        out_specs=pl.BlockSpec((tm, tn), lambda i, j: (i, j)),
        out_shape=jax.ShapeDtypeStruct((m, n), out_dtype),
        compiler_params=_params("parallel", "arbitrary"),
        name=name,
    )(a, wt)


def _ba_kernel(heads, h_ref, w_ref, alog_ref, dtb_ref, o_ref):
    p = _dot_nt(h_ref[...], w_ref[...].astype(BF16))
    lane = lax.broadcasted_iota(jnp.int32, p.shape, 1)
    is_g = (lane >= heads) & (lane < 2 * heads)
    g = jnp.where(is_g, -jnp.exp(alog_ref[...]) * _softplus(p + dtb_ref[...]), 0.0)
    pos = lax.broadcasted_iota(jnp.int32, p.shape, 0) % CHUNK
    shift = 1
    while shift < CHUNK:
        g = g + jnp.where(pos >= shift, pltpu.roll(g, shift, axis=0), 0.0)
        shift *= 2
    o_ref[...] = jnp.where(is_g, g, _sigmoid(p))


def _ba(h, wt, layer, row0, a_log, dt_bias, heads):
    m, k = h.shape
    tm = min(512, m)
    assert row0 % LANES == 0 and 2 * heads <= LANES
    pad = lambda v: jnp.zeros((1, LANES), F32).at[0, heads:2 * heads].set(v)
    return pl.pallas_call(
        functools.partial(_ba_kernel, heads),
        grid=(m // tm,),
        in_specs=[pl.BlockSpec((tm, k), lambda i: (i, 0)),
                  pl.BlockSpec((None, LANES, k), lambda i: (layer, row0 // LANES, 0)),
                  pl.BlockSpec((1, LANES), lambda i: (0, 0)),
                  pl.BlockSpec((1, LANES), lambda i: (0, 0))],
        out_specs=pl.BlockSpec((tm, LANES), lambda i: (i, 0)),
        out_shape=jax.ShapeDtypeStruct((m, LANES), F32),
        compiler_params=_params("parallel"),
        name="dn_gates",
    )(h, wt, pad(a_log), pad(dt_bias))


def _band(tm, lo, hi):
    ri = lax.broadcasted_iota(jnp.int32, (tm, tm + HALO), 0)
    ci = lax.broadcasted_iota(jnp.int32, (tm, tm + HALO), 1)
    back = ri + HALO - ci
    return jnp.where((back >= lo) & (back <= hi), 1.0, 0.0).astype(BF16)


def _pool_kernel(per_seq, u_ref, halo_ref, pw_ref, sc_ref, o_ref, ext_ref):
    tm, width = u_ref.shape
    group = width // len(POOL_WINDOWS)
    t = pl.program_id(0) % per_seq
    ext_ref[0:HALO, :] = jnp.where(t == 0, jnp.zeros(halo_ref.shape, halo_ref.dtype), halo_ref[...])
    ext_ref[HALO:, :] = u_ref[...]
    pos = t * tm + lax.broadcasted_iota(jnp.int32, (tm, 1), 0)
    for gi, win in enumerate(POOL_WINDOWS):
        cols = slice(gi * group, (gi + 1) * group)
        window_sum = _dot(_band(tm, 0, win - 1), ext_ref[:, cols])
        inv_count = 1.0 / jnp.minimum(pos + 1, win).astype(F32)
        pa = window_sum * inv_count - u_ref[:, cols].astype(F32)
        r = _dot(pa.astype(BF16), pw_ref[gi].astype(BF16)) * sc_ref[:, cols]
        o_ref[:, cols] = r.astype(o_ref.dtype)


def _pool(proj, pool_w, layer, pool_scale, seq):
    m = proj.shape[0]
    _, groups, group, _ = pool_w.shape
    width = groups * group
    tm = min(256, seq)
    per_seq = seq // tm
    return pl.pallas_call(
        functools.partial(_pool_kernel, per_seq),
        grid=(m // tm,),
        in_specs=[pl.BlockSpec((tm, width), lambda i: (i, 0)),
                  pl.BlockSpec((HALO, width), lambda i: (jnp.maximum(i * (tm // HALO) - 1, 0), 0)),
                  pl.BlockSpec((None, groups, group, group), lambda i: (layer, 0, 0, 0)),
                  pl.BlockSpec((1, width), lambda i: (0, 0))],
        out_specs=pl.BlockSpec((tm, width), lambda i: (i, 0)),
        out_shape=jax.ShapeDtypeStruct((m, width), BF16),
        scratch_shapes=[pltpu.VMEM((tm + HALO, width), BF16)],
        compiler_params=_params("parallel"),
        name="pool_mixer",
    )(proj, proj, pool_w, pool_scale.reshape(1, width))


def _qkv_kernel(per_seq, q_ref, k_ref, v_ref, hq_ref, hk_ref, hv_ref, wq_ref, wk_ref, wv_ref,
                qo_ref, ko_ref, vo_ref, ext_ref):
    tm, width = q_ref.shape
    first = pl.program_id(0) % per_seq == 0
    taps = DN_CONV_TAPS
    shifts = jnp.concatenate([_band(tm, s, s) for s in range(1, taps)], axis=0)
    chunk = 2 * HEAD_DIM
    for x_ref, halo_ref, w_ref, o_ref, norm in (
            (q_ref, hq_ref, wq_ref, qo_ref, HEAD_DIM ** -0.5),
            (k_ref, hk_ref, wk_ref, ko_ref, 1.0),
            (v_ref, hv_ref, wv_ref, vo_ref, None)):
        ext_ref[0:HALO, :] = jnp.where(first, jnp.zeros(halo_ref.shape, halo_ref.dtype), halo_ref[...])
        ext_ref[HALO:, :] = x_ref[...]
        for c0 in range(0, width, chunk):
            wide = slice(c0, c0 + chunk)
            back = _dot(shifts, ext_ref[:, wide])
            y = x_ref[:, wide].astype(F32) * w_ref[taps - 1:taps, wide]
            for s in range(1, taps):
                y = y + back[(s - 1) * tm:s * tm] * w_ref[taps - 1 - s:taps - s, wide]
            y = _silu(y)
            for c in range(0, chunk, HEAD_DIM):
                yh = y[:, c:c + HEAD_DIM]
                if norm is not None:
                    yh = yh * (lax.rsqrt(jnp.sum(yh * yh, axis=-1, keepdims=True) + EPS) * norm)
                o_ref[:, c0 + c:c0 + c + HEAD_DIM] = yh.astype(o_ref.dtype)


def _qkv_conv(proj, dn_conv, width, col0, seq):
    m = proj.shape[0]
    tm = min(128, seq)
    per_seq = seq // tm
    b0 = col0 // width
    rows = [pl.BlockSpec((tm, width), functools.partial(lambda s, i: (i, b0 + s), s)) for s in range(3)]
    halos = [pl.BlockSpec((HALO, width),
                          functools.partial(lambda s, i: (jnp.maximum(i * (tm // HALO) - 1, 0), b0 + s), s))
             for s in range(3)]
    taps = [pl.BlockSpec((DN_CONV_TAPS, width), functools.partial(lambda s, i: (0, s), s)) for s in range(3)]
    out = pl.BlockSpec((tm, width), lambda i: (i, 0))
    shp = jax.ShapeDtypeStruct((m, width), BF16)
    return pl.pallas_call(
        functools.partial(_qkv_kernel, per_seq),
        grid=(m // tm,),
        in_specs=rows + halos + taps,
        out_specs=[out, out, out],
        out_shape=[shp, shp, shp],
        scratch_shapes=[pltpu.VMEM((tm + HALO, width), BF16)],
        compiler_params=_params("parallel"),
        name="dn_qkv_conv",
    )(proj, proj, proj, proj, proj, proj, dn_conv, dn_conv, dn_conv)


def _unit_lower_inverses(lows):
    n = lows[0].shape[0]
    ri = lax.broadcasted_iota(jnp.int32, (n, n), 0)
    ci = lax.broadcasted_iota(jnp.int32, (n, n), 1)
    same = lambda bits: jnp.right_shift(ri, bits) == jnp.right_shift(ci, bits)
    eye = jnp.where(ri == ci, 1.0, 0.0)
    invs = [eye - jnp.where(same(1), low, 0.0) for low in lows]
    bits = 1
    while (1 << bits) < n:
        mask = same(bits + 1) & jnp.logical_not(same(bits))
        inv16 = [inv.astype(BF16) for inv in invs]
        right = [_dot(jnp.where(mask, low, 0.0).astype(BF16), i16) for low, i16 in zip(lows, inv16)]
        invs = [inv - _dot(i16, r.astype(BF16)) for inv, i16, r in zip(invs, inv16, right)]
        bits += 1
    return invs


def _delta_kernel(heads, q_ref, k_ref, v_ref, z_ref, g_ref, gt_ref, nw_ref, o_ref, state_ref):
    rows, width = q_ref.shape
    hp = width // HEAD_DIM
    c = CHUNK

    @pl.when(pl.program_id(2) == 0)
    def _():
        state_ref[...] = jnp.zeros_like(state_ref)

    gates = g_ref[...]
    lane = lax.broadcasted_iota(jnp.int32, gates.shape, 1)
    ri = lax.broadcasted_iota(jnp.int32, (c, c), 0)
    ci = lax.broadcasted_iota(jnp.int32, (c, c), 1)
    causal = ri >= ci
    strict = ri > ci

    work = []
    for hh in range(hp):
        head = pl.program_id(1) * hp + hh
        cols = slice(hh * HEAD_DIM, (hh + 1) * HEAD_DIM)
        beta_col = jnp.sum(jnp.where(lane == head, gates, 0.0), axis=-1, keepdims=True)
        gc_col = jnp.sum(jnp.where(lane == heads + head, gates, 0.0), axis=-1, keepdims=True)
        gc_row = gt_ref[pl.ds(heads + head, 1), :]
        for r0 in range(0, rows, c):
            rsl = slice(r0, r0 + c)
            q = q_ref[rsl, cols]
            k16 = k_ref[rsl, cols]
            k = k16.astype(F32)
            beta = beta_col[rsl]
            gcc = gc_col[rsl]
            gcr = gc_row[:, rsl]
            g_last = gcr[:, c - 1:c]
            decay = jnp.exp(jnp.where(causal, gcc - gcr, -jnp.inf))
            e_g = jnp.exp(gcc)
            kb = k * beta
            kq_k = _dot_nt(jnp.concatenate([kb.astype(BF16), q], axis=0), k16)
            low = jnp.where(strict, kq_k[:c] * decay, 0.0)
            qk = jnp.where(causal, kq_k[c:] * decay, 0.0)
            work.append(dict(
                hh=hh, rsl=rsl, cols=cols, low=low,
                vb_kbg16=jnp.concatenate([(v_ref[rsl, cols].astype(F32) * beta).astype(BF16),
                                          (kb * e_g).astype(BF16)], axis=1),
                qd16=(q.astype(F32) * e_g).astype(BF16),
                qk_kdt16=jnp.concatenate([qk.astype(BF16),
                                          (k * jnp.exp(g_last - gcc)).T.astype(BF16)], axis=0),
                g_end=jnp.exp(g_last)))
    for item, t_inv in zip(work, _unit_lower_inverses([item["low"] for item in work])):
        uw = _dot(t_inv.astype(BF16), item["vb_kbg16"])
        item["u"] = uw[:, :HEAD_DIM]
        item["w_qd16"] = jnp.concatenate([uw[:, HEAD_DIM:].astype(BF16), item["qd16"]], axis=0)

    for hh in range(hp):
        state = state_ref[hh]
        for item in work:
            if item["hh"] != hh:
                continue
            from_state = _dot(item["w_qd16"], state.astype(BF16))
            vn16 = (item["u"] - from_state[:c]).astype(BF16)
            from_new = _dot(item["qk_kdt16"], vn16)
            item["o"] = from_state[c:] + from_new[:c]
            state = state * item["g_end"] + from_new[c:]
        state_ref[hh] = state

    for item in work:
        rsl, cols = item["rsl"], item["cols"]
        z = z_ref[rsl, cols].astype(F32)
        o_ref[rsl, cols] = (_rms(item["o"]) * nw_ref[...] * _silu(z)).astype(o_ref.dtype)


def _delta(q, k, v, proj, z_col0, gates, gates_t, dn_norm, batch, seq, heads):
    m, width = q.shape
    hp = min(16, heads)
    rows = min(CHUNK, seq)
    per_seq = seq // rows
    bw = hp * HEAD_DIM
    zb0 = z_col0 // bw
    qkv = pl.BlockSpec((rows, bw), lambda b, h, t: (b * per_seq + t, h))
    return pl.pallas_call(
        functools.partial(_delta_kernel, heads),
        grid=(batch, heads // hp, per_seq),
        in_specs=[qkv, qkv, qkv,
                  pl.BlockSpec((rows, bw), lambda b, h, t: (b * per_seq + t, zb0 + h)),
                  pl.BlockSpec((rows, LANES), lambda b, h, t: (b * per_seq + t, 0)),
                  pl.BlockSpec((2 * heads, rows), lambda b, h, t: (0, b * per_seq + t)),
                  pl.BlockSpec((1, HEAD_DIM), lambda b, h, t: (0, 0))],
        out_specs=qkv,
        out_shape=jax.ShapeDtypeStruct((m, width), BF16),
        scratch_shapes=[pltpu.VMEM((hp, HEAD_DIM, HEAD_DIM), F32)],
        compiler_params=_params("parallel", "parallel", "arbitrary"),
        name="gated_delta",
    )(q, k, v, proj, gates, gates_t, dn_norm.reshape(1, HEAD_DIM))


def _merge_kernel(ga_ref, gb_ref, pa_ref, ob_ref, pp_ref, dp_ref, y_ref):
    ya = _sigmoid(ga_ref[...].astype(F32)) * _dot(pa_ref[...], pp_ref[...].astype(BF16))
    yb = _sigmoid(gb_ref[...].astype(F32)) * _dot(ob_ref[...], dp_ref[...].astype(BF16))
    y_ref[...] = (ya + yb).astype(y_ref.dtype)


def _merge(proj, gate_col0, pa, ob, pool_proj, dn_proj, layer):
    m, kp = pa.shape
    d = pool_proj.shape[2]
    tm, tn = min(1024, m), min(512, d)
    nb = d // tn
    assert gate_col0 % tn == 0
    g0 = gate_col0 // tn
    return pl.pallas_call(
        _merge_kernel,
        grid=(nb, m // tm),
        in_specs=[pl.BlockSpec((tm, tn), lambda j, i: (i, g0 + j)),
                  pl.BlockSpec((tm, tn), lambda j, i: (i, g0 + nb + j)),
                  pl.BlockSpec((tm, kp), lambda j, i: (i, 0)),
                  pl.BlockSpec((tm, kp), lambda j, i: (i, 0)),
                  pl.BlockSpec((None, kp, tn), lambda j, i: (layer, 0, j)),
                  pl.BlockSpec((None, kp, tn), lambda j, i: (layer, 0, j))],
        out_specs=pl.BlockSpec((tm, tn), lambda j, i: (i, j)),
        out_shape=jax.ShapeDtypeStruct((m, d), BF16),
        compiler_params=_params("parallel", "arbitrary"),
        name="gated_merge",
    )(proj, proj, pa, ob, pool_proj, dn_proj)


FFN_CARRY = 8


def _ffn_up_kernel(per_seq, h_ref, wa_ref, wu0_ref, wu1_ref, cw_ref, cb_ref, o_ref, a_ref):
    tm = h_ref.shape[0]

    @pl.when(pl.program_id(1) % per_seq == 0)
    def _():
        a_ref[0:FFN_CARRY, :] = jnp.zeros((FFN_CARRY, a_ref.shape[1]), F32)

    h = h_ref[...]
    a_ref[FFN_CARRY:, :] = _dot(h, wa_ref[...])
    u = jnp.concatenate([_dot(h, wu0_ref[...]), _dot(h, wu1_ref[...])], axis=1)
    a = cb_ref[...]
    for j in range(FFN_CONV_TAPS):
        r0 = FFN_CARRY - (FFN_CONV_TAPS - 1) + j
        a = a + a_ref[r0:r0 + tm, :] * cw_ref[j:j + 1, :]
    o_ref[...] = (_silu(a) * u).astype(o_ref.dtype)
    a_ref[0:FFN_CARRY, :] = a_ref[tm:tm + FFN_CARRY, :]


def _ffn_up(h, w_up, layer, conv_w, conv_b, seq):
    m, d = h.shape
    n = w_up.shape[2] // 2
    tm, tn = min(1024, seq), 512
    half = tn // 2
    assert n % half == 0
    nh = n // half
    nt = pl.cdiv(n, tn)
    per_seq = seq // tm
    pad = nt * tn - n
    conv_w = jnp.pad(conv_w, ((0, 0), (0, 0), (0, pad)))
    conv_b = jnp.pad(conv_b, ((0, 0), (0, pad))).reshape(conv_b.shape[0], 1, nt * tn)
    return pl.pallas_call(
        functools.partial(_ffn_up_kernel, per_seq),
        grid=(nt, m // tm),
        in_specs=[pl.BlockSpec((tm, d), lambda j, i: (i, 0)),
                  pl.BlockSpec((None, d, tn), lambda j, i: (layer, 0, j)),
                  pl.BlockSpec((None, d, half), lambda j, i: (layer, 0, nh + 2 * j)),
                  pl.BlockSpec((None, d, half), lambda j, i: (layer, 0, jnp.minimum(nh + 2 * j + 1, 2 * nh - 1))),
                  pl.BlockSpec((None, FFN_CONV_TAPS, tn), lambda j, i: (layer, 0, j)),
                  pl.BlockSpec((None, 1, tn), lambda j, i: (layer, 0, j))],
        out_specs=pl.BlockSpec((tm, tn), lambda j, i: (i, j)),
        out_shape=jax.ShapeDtypeStruct((m, n), BF16),
        scratch_shapes=[pltpu.VMEM((tm + FFN_CARRY, tn), F32)],
        compiler_params=_params("arbitrary", "arbitrary"),
        name="ffn_up_conv_gate",
    )(h, w_up, w_up, w_up, conv_w, conv_b)


def kernel(x, c, w_ada, b_ada, ada_table, mix_pre_norm, w_in, pool_w, pool_scale, pool_proj,
           dn_conv, dn_a_log, dn_dt_bias, dn_norm, dn_proj, w_o, mix_post_norm,
           ffn_pre_norm, w_up, ffn_conv, ffn_conv_b, w_down, ffn_post_norm):
    batch, seq, d = x.shape
    depth = w_in.shape[0]
    m = batch * seq
    pool_width = pool_scale.shape[1]
    dn_width = dn_proj.shape[1]
    heads = dn_a_log.shape[1]
    c_qkv = pool_width
    c_z = c_qkv + 3 * dn_width
    c_ba = c_z + dn_width

    c8 = jnp.pad(c, ((0, 8 - batch), (0, 0)))
    mod = _ada(c8, w_ada, b_ada, ada_table.reshape(depth, N_MOD * d))
    mod = mod[:, :batch].reshape(depth, batch, N_MOD, 1, d)

    xf = x.reshape(m, d)
    h = _norm_mod(xf, mix_pre_norm[0], mod[0, :, 1], mod[0, :, 0], seq)
    w_up16, w_down16 = w_up.astype(BF16), w_down.astype(BF16)
    w_in_t = jnp.swapaxes(w_in, 1, 2)
    for l in range(depth):
        shift_f, scale_f, gate_m, gate_f = mod[l, :, 3], mod[l, :, 4], mod[l, :, 2], mod[l, :, 5]
        proj = _matmul_nt(h, w_in_t, l, c_ba + 2 * d, c_ba, 2 * heads, BF16, 1024, 512, "in_proj")
        gates = _ba(h, w_in_t, l, c_ba, dn_a_log[l], dn_dt_bias[l], heads)
        gates_t = gates[:, :2 * heads].T
        pa = _pool(proj, pool_w, l, pool_scale[l], seq)
        q, k, v = _qkv_conv(proj, dn_conv[l], dn_width, c_qkv, seq)
        ob = _delta(q, k, v, proj, c_z, gates, gates_t, dn_norm[l], batch, seq, heads)
        y = _merge(proj, c_ba, pa, ob, pool_proj, dn_proj, l)
        yo = _matmul(y, w_o, l, d, BF16, 1024, 512, "out_proj", rows_outer=True)
        xf, h = _resid_norm(yo, xf, mix_post_norm[l], gate_m, seq,
                            nxt=(ffn_pre_norm[l], scale_f, shift_f))
        act = _ffn_up(h, w_up16, l, ffn_conv, ffn_conv_b, seq)
        yd = _matmul(act, w_down16, l, d, BF16, 512, 512, "ffn_down")
        if l + 1 < depth:
            xf, h = _resid_norm(yd, xf, ffn_post_norm[l], gate_f, seq,
                                nxt=(mix_pre_norm[l + 1], mod[l + 1, :, 1], mod[l + 1, :, 0]))
        else:
            xf = _resid_norm(yd, xf, ffn_post_norm[l], gate_f, seq)
    return xf.reshape(batch, seq, d)
```

```python
import functools

import jax
import jax.numpy as jnp
from jax import lax
from jax.experimental import pallas as pl
from jax.experimental.pallas import tpu as pltpu

F32 = jnp.float32
BF16 = jnp.bfloat16

EPS = 1e-6
POOL_WINDOWS = (2, 4, 8, 16)
HEAD_DIM = 128
DN_CONV_TAPS = 4
FFN_CONV_TAPS = 3
N_MOD = 6
CHUNK = 128
HALO = 16
VMEM_LIMIT_BYTES = 56 * 1024 * 1024
LANES = 128


def _params(*semantics):
    return pltpu.CompilerParams(dimension_semantics=semantics,
                                vmem_limit_bytes=VMEM_LIMIT_BYTES)


def _dot(a, b):
    return jnp.dot(a, b, preferred_element_type=F32)


def _dot_nt(a, b):
    return lax.dot_general(a, b, (((1,), (1,)), ((), ())), preferred_element_type=F32)


def _sigmoid(x):
    return 1.0 / (1.0 + jnp.exp(-x))


def _silu(x):
    return x * _sigmoid(x)


def _softplus(x):
    return jnp.maximum(x, 0.0) + jnp.log1p(jnp.exp(-jnp.abs(x)))


def _rms(x):
    return x * lax.rsqrt(jnp.mean(x * x, axis=-1, keepdims=True) + EPS)


def _ada_kernel(c_ref, w_ref, b_ref, t_ref, o_ref):
    a = _silu(c_ref[...]).astype(BF16)
    acc = _dot(a, w_ref[...].astype(BF16)) + b_ref[...]
    o_ref[...] = acc[None] + t_ref[...]


def _ada(c8, w_ada, b_ada, table):
    depth = table.shape[0]
    d, n = w_ada.shape
    tn = min(512, n)
    return pl.pallas_call(
        _ada_kernel,
        grid=(n // tn,),
        in_specs=[pl.BlockSpec((8, d), lambda j: (0, 0)),
                  pl.BlockSpec((d, tn), lambda j: (0, j)),
                  pl.BlockSpec((1, tn), lambda j: (0, j)),
                  pl.BlockSpec((depth, 1, tn), lambda j: (0, 0, j))],
        out_specs=pl.BlockSpec((depth, 8, tn), lambda j: (0, 0, j)),
        out_shape=jax.ShapeDtypeStruct((depth, 8, n), F32),
        compiler_params=_params("arbitrary"),
        name="ada",
    )(c8, w_ada, b_ada.reshape(1, n), table.reshape(depth, 1, n))


def _norm_mod_kernel(x_ref, w_ref, sc_ref, sh_ref, h_ref):
    y = _rms(x_ref[...]) * w_ref[...]
    h_ref[...] = (y * (1.0 + sc_ref[0]) + sh_ref[0]).astype(h_ref.dtype)


def _norm_mod(x, w, scale, shift, seq):
    m, d = x.shape
    tm = min(256, seq)
    per_seq = seq // tm
    row = pl.BlockSpec((tm, d), lambda i: (i, 0))
    vec = pl.BlockSpec((1, d), lambda i: (0, 0))
    bvec = pl.BlockSpec((1, 1, d), lambda i: (i // per_seq, 0, 0))
    return pl.pallas_call(
        _norm_mod_kernel,
        grid=(m // tm,),
        in_specs=[row, vec, bvec, bvec],
        out_specs=row,
        out_shape=jax.ShapeDtypeStruct((m, d), BF16),
        compiler_params=_params("parallel"),
        name="norm_mod",
    )(x, w.reshape(1, d), scale, shift)


def _resid_norm_kernel(y_ref, x_ref, pw_ref, g_ref, nw_ref, sc_ref, sh_ref, xo_ref, h_ref):
    xn = x_ref[...] + g_ref[0] * (_rms(y_ref[...].astype(F32)) * pw_ref[...])
    xo_ref[...] = xn
    h = _rms(xn) * nw_ref[...]
    h_ref[...] = (h * (1.0 + sc_ref[0]) + sh_ref[0]).astype(h_ref.dtype)


def _resid_kernel(y_ref, x_ref, pw_ref, g_ref, xo_ref):
    xo_ref[...] = x_ref[...] + g_ref[0] * (_rms(y_ref[...].astype(F32)) * pw_ref[...])


def _resid_norm(y, x, post_w, gate, seq, nxt=None):
    m, d = x.shape
    tm = min(256, seq)
    per_seq = seq // tm
    row = pl.BlockSpec((tm, d), lambda i: (i, 0))
    vec = pl.BlockSpec((1, d), lambda i: (0, 0))
    bvec = pl.BlockSpec((1, 1, d), lambda i: (i // per_seq, 0, 0))
    if nxt is None:
        return pl.pallas_call(
            _resid_kernel,
            grid=(m // tm,),
            in_specs=[row, row, vec, bvec],
            out_specs=row,
            out_shape=jax.ShapeDtypeStruct((m, d), F32),
            compiler_params=_params("parallel"),
            name="resid",
        )(y, x, post_w.reshape(1, d), gate)
    norm_w, scale, shift = nxt
    return pl.pallas_call(
        _resid_norm_kernel,
        grid=(m // tm,),
        in_specs=[row, row, vec, bvec, vec, bvec, bvec],
        out_specs=[row, row],
        out_shape=[jax.ShapeDtypeStruct((m, d), F32), jax.ShapeDtypeStruct((m, d), BF16)],
        compiler_params=_params("parallel"),
        name="resid_norm",
    )(y, x, post_w.reshape(1, d), gate, norm_w.reshape(1, d), scale, shift)


def _mm_kernel(a_ref, w_ref, o_ref):
    o_ref[...] = _dot(a_ref[...], w_ref[...].astype(BF16)).astype(o_ref.dtype)


def _matmul(a, w, layer, n, out_dtype, tm, tn, name, rows_outer=False):
    m, k = a.shape
    tm, tn = min(tm, m), min(tn, n)
    if rows_outer:
        grid, ij = (m // tm, n // tn), lambda i, j: (i, j)
    else:
        grid, ij = (n // tn, m // tm), lambda j, i: (i, j)
    return pl.pallas_call(
        _mm_kernel,
        grid=grid,
        in_specs=[pl.BlockSpec((tm, k), lambda *g: (ij(*g)[0], 0)),
                  pl.BlockSpec((None, k, tn), lambda *g: (layer, 0, ij(*g)[1]))],
        out_specs=pl.BlockSpec((tm, tn), lambda *g: ij(*g)),
        out_shape=jax.ShapeDtypeStruct((m, n), out_dtype),
        compiler_params=_params("parallel", "arbitrary"),
        name=name,
    )(a, w)


def _mm_nt_kernel(a_ref, wt_ref, o_ref):
    o_ref[...] = _dot_nt(a_ref[...], wt_ref[0].astype(BF16)).astype(o_ref.dtype)


def _matmul_nt(a, wt, layer, n, skip_at, skip, out_dtype, tm, tn, name):
    m, k = a.shape
    tm, tn = min(tm, m), min(tn, n)
    sublanes = 8 * 4 // wt.dtype.itemsize
    assert skip % sublanes == 0 and n % tn == 0 and skip_at % tn == 0
    first_after = skip_at // tn
    row = lambda j: pl.multiple_of(j * tn + jnp.where(j >= first_after, skip, 0), sublanes)
    return pl.pallas_call(
        _mm_nt_kernel,
        grid=(m // tm, n // tn),
        in_specs=[pl.BlockSpec((tm, k), lambda i, j: (i, 0)),
                  pl.BlockSpec((pl.Element(1), pl.Element(tn), pl.Element(k)),
                               lambda i, j: (layer, row(j), 0))],
        out_specs=pl.BlockSpec((tm, tn), lambda i, j: (i, j)),
        out_shape=jax.ShapeDtypeStruct((m, n), out_dtype),
        compiler_params=_params("parallel", "arbitrary"),
        name=name,
    )(a, wt)


def _ba_kernel(heads, h_ref, w_ref, alog_ref, dtb_ref, o_ref):
    p = _dot_nt(h_ref[...], w_ref[...].astype(BF16))
    lane = lax.broadcasted_iota(jnp.int32, p.shape, 1)
    is_g = (lane >= heads) & (lane < 2 * heads)
    g = jnp.where(is_g, -jnp.exp(alog_ref[...]) * _softplus(p + dtb_ref[...]), 0.0)
    pos = lax.broadcasted_iota(jnp.int32, p.shape, 0) % CHUNK
    shift = 1
    while shift < CHUNK:
        g = g + jnp.where(pos >= shift, pltpu.roll(g, shift, axis=0), 0.0)
        shift *= 2
    o_ref[...] = jnp.where(is_g, g, _sigmoid(p))


def _ba(h, wt, layer, row0, a_log, dt_bias, heads):
    m, k = h.shape
    tm = min(512, m)
    assert row0 % LANES == 0 and 2 * heads <= LANES
    pad = lambda v: jnp.zeros((1, LANES), F32).at[0, heads:2 * heads].set(v)
    return pl.pallas_call(
        functools.partial(_ba_kernel, heads),
        grid=(m // tm,),
        in_specs=[pl.BlockSpec((tm, k), lambda i: (i, 0)),
                  pl.BlockSpec((None, LANES, k), lambda i: (layer, row0 // LANES, 0)),
                  pl.BlockSpec((1, LANES), lambda i: (0, 0)),
                  pl.BlockSpec((1, LANES), lambda i: (0, 0))],
        out_specs=pl.BlockSpec((tm, LANES), lambda i: (i, 0)),
        out_shape=jax.ShapeDtypeStruct((m, LANES), F32),
        compiler_params=_params("parallel"),
        name="dn_gates",
    )(h, wt, pad(a_log), pad(dt_bias))


def _band(tm, lo, hi):
    ri = lax.broadcasted_iota(jnp.int32, (tm, tm + HALO), 0)
    ci = lax.broadcasted_iota(jnp.int32, (tm, tm + HALO), 1)
    back = ri + HALO - ci
    return jnp.where((back >= lo) & (back <= hi), 1.0, 0.0).astype(BF16)


def _pool_kernel(per_seq, u_ref, halo_ref, pw_ref, sc_ref, o_ref, ext_ref):
    tm, width = u_ref.shape
    group = width // len(POOL_WINDOWS)
    t = pl.program_id(0) % per_seq
    ext_ref[0:HALO, :] = jnp.where(t == 0, jnp.zeros(halo_ref.shape, halo_ref.dtype), halo_ref[...])
    ext_ref[HALO:, :] = u_ref[...]
    pos = t * tm + lax.broadcasted_iota(jnp.int32, (tm, 1), 0)
    for gi, win in enumerate(POOL_WINDOWS):
        cols = slice(gi * group, (gi + 1) * group)
        window_sum = _dot(_band(tm, 0, win - 1), ext_ref[:, cols])
        inv_count = 1.0 / jnp.minimum(pos + 1, win).astype(F32)
        pa = window_sum * inv_count - u_ref[:, cols].astype(F32)
        r = _dot(pa.astype(BF16), pw_ref[gi].astype(BF16)) * sc_ref[:, cols]
        o_ref[:, cols] = r.astype(o_ref.dtype)


def _pool(proj, pool_w, layer, pool_scale, seq):
    m = proj.shape[0]
    _, groups, group, _ = pool_w.shape
    width = groups * group
    tm = min(256, seq)
    per_seq = seq // tm
    return pl.pallas_call(
        functools.partial(_pool_kernel, per_seq),
        grid=(m // tm,),
        in_specs=[pl.BlockSpec((tm, width), lambda i: (i, 0)),
                  pl.BlockSpec((HALO, width), lambda i: (jnp.maximum(i * (tm // HALO) - 1, 0), 0)),
                  pl.BlockSpec((None, groups, group, group), lambda i: (layer, 0, 0, 0)),
                  pl.BlockSpec((1, width), lambda i: (0, 0))],
        out_specs=pl.BlockSpec((tm, width), lambda i: (i, 0)),
        out_shape=jax.ShapeDtypeStruct((m, width), BF16),
        scratch_shapes=[pltpu.VMEM((tm + HALO, width), BF16)],
        compiler_params=_params("parallel"),
        name="pool_mixer",
    )(proj, proj, pool_w, pool_scale.reshape(1, width))


def _qkv_kernel(per_seq, q_ref, k_ref, v_ref, hq_ref, hk_ref, hv_ref, wq_ref, wk_ref, wv_ref,
                qo_ref, ko_ref, vo_ref, ext_ref):
    tm, width = q_ref.shape
    first = pl.program_id(0) % per_seq == 0
    taps = DN_CONV_TAPS
    shifts = jnp.concatenate([_band(tm, s, s) for s in range(1, taps)], axis=0)
    chunk = 2 * HEAD_DIM
    for x_ref, halo_ref, w_ref, o_ref, norm in (
            (q_ref, hq_ref, wq_ref, qo_ref, HEAD_DIM ** -0.5),
            (k_ref, hk_ref, wk_ref, ko_ref, 1.0),
            (v_ref, hv_ref, wv_ref, vo_ref, None)):
        ext_ref[0:HALO, :] = jnp.where(first, jnp.zeros(halo_ref.shape, halo_ref.dtype), halo_ref[...])
        ext_ref[HALO:, :] = x_ref[...]
        for c0 in range(0, width, chunk):
            wide = slice(c0, c0 + chunk)
            back = _dot(shifts, ext_ref[:, wide])
            y = x_ref[:, wide].astype(F32) * w_ref[taps - 1:taps, wide]
            for s in range(1, taps):
                y = y + back[(s - 1) * tm:s * tm] * w_ref[taps - 1 - s:taps - s, wide]
            y = _silu(y)
            for c in range(0, chunk, HEAD_DIM):
                yh = y[:, c:c + HEAD_DIM]
                if norm is not None:
                    yh = yh * (lax.rsqrt(jnp.sum(yh * yh, axis=-1, keepdims=True) + EPS) * norm)
                o_ref[:, c0 + c:c0 + c + HEAD_DIM] = yh.astype(o_ref.dtype)


def _qkv_conv(proj, dn_conv, width, col0, seq):
    m = proj.shape[0]
    tm = min(128, seq)
    per_seq = seq // tm
    b0 = col0 // width
    rows = [pl.BlockSpec((tm, width), functools.partial(lambda s, i: (i, b0 + s), s)) for s in range(3)]
    halos = [pl.BlockSpec((HALO, width),
                          functools.partial(lambda s, i: (jnp.maximum(i * (tm // HALO) - 1, 0), b0 + s), s))
             for s in range(3)]
    taps = [pl.BlockSpec((DN_CONV_TAPS, width), functools.partial(lambda s, i: (0, s), s)) for s in range(3)]
    out = pl.BlockSpec((tm, width), lambda i: (i, 0))
    shp = jax.ShapeDtypeStruct((m, width), BF16)
    return pl.pallas_call(
        functools.partial(_qkv_kernel, per_seq),
        grid=(m // tm,),
        in_specs=rows + halos + taps,
        out_specs=[out, out, out],
        out_shape=[shp, shp, shp],
        scratch_shapes=[pltpu.VMEM((tm + HALO, width), BF16)],
        compiler_params=_params("parallel"),
        name="dn_qkv_conv",
    )(proj, proj, proj, proj, proj, proj, dn_conv, dn_conv, dn_conv)


def _unit_lower_inverses(lows):
    n = lows[0].shape[0]
    ri = lax.broadcasted_iota(jnp.int32, (n, n), 0)
    ci = lax.broadcasted_iota(jnp.int32, (n, n), 1)
    same = lambda bits: jnp.right_shift(ri, bits) == jnp.right_shift(ci, bits)
    eye = jnp.where(ri == ci, 1.0, 0.0)
    invs = [eye - jnp.where(same(1), low, 0.0) for low in lows]
    bits = 1
    while (1 << bits) < n:
        mask = same(bits + 1) & jnp.logical_not(same(bits))
        inv16 = [inv.astype(BF16) for inv in invs]
        right = [_dot(jnp.where(mask, low, 0.0).astype(BF16), i16) for low, i16 in zip(lows, inv16)]
        invs = [inv - _dot(i16, r.astype(BF16)) for inv, i16, r in zip(invs, inv16, right)]
        bits += 1
    return invs


def _ride_along_blocks(rows, steps):
    block = next(b for b in range(16, rows + 1, 16) if rows % b == 0 and rows // b <= steps)
    return block, rows // block


def _delta_kernel(heads, q_ref, k_ref, v_ref, z_ref, g_ref, gt_ref, nw_ref, wsrc_ref, o_ref, w16_ref,
                  state_ref):
    rows, width = q_ref.shape
    hp = width // HEAD_DIM
    c = CHUNK
    w16_ref[...] = wsrc_ref[...].astype(BF16)

    @pl.when(pl.program_id(2) == 0)
    def _():
        state_ref[...] = jnp.zeros_like(state_ref)

    gates = g_ref[...]
    lane = lax.broadcasted_iota(jnp.int32, gates.shape, 1)
    ri = lax.broadcasted_iota(jnp.int32, (c, c), 0)
    ci = lax.broadcasted_iota(jnp.int32, (c, c), 1)
    causal = ri >= ci
    strict = ri > ci

    work = []
    for hh in range(hp):
        head = pl.program_id(1) * hp + hh
        cols = slice(hh * HEAD_DIM, (hh + 1) * HEAD_DIM)
        beta_col = jnp.sum(jnp.where(lane == head, gates, 0.0), axis=-1, keepdims=True)
        gc_col = jnp.sum(jnp.where(lane == heads + head, gates, 0.0), axis=-1, keepdims=True)
        gc_row = gt_ref[pl.ds(heads + head, 1), :]
        for r0 in range(0, rows, c):
            rsl = slice(r0, r0 + c)
            q = q_ref[rsl, cols]
            k16 = k_ref[rsl, cols]
            k = k16.astype(F32)
            beta = beta_col[rsl]
            gcc = gc_col[rsl]
            gcr = gc_row[:, rsl]
            g_last = gcr[:, c - 1:c]
            decay = jnp.exp(jnp.where(causal, gcc - gcr, -jnp.inf))
            e_g = jnp.exp(gcc)
            kb = k * beta
            kq_k = _dot_nt(jnp.concatenate([kb.astype(BF16), q], axis=0), k16)
            low = jnp.where(strict, kq_k[:c] * decay, 0.0)
            qk = jnp.where(causal, kq_k[c:] * decay, 0.0)
            work.append(dict(
                hh=hh, rsl=rsl, cols=cols, low=low,
                vb_kbg16=jnp.concatenate([(v_ref[rsl, cols].astype(F32) * beta).astype(BF16),
                                          (kb * e_g).astype(BF16)], axis=1),
                qd16=(q.astype(F32) * e_g).astype(BF16),
                qk_kdt16=jnp.concatenate([qk.astype(BF16),
                                          (k * jnp.exp(g_last - gcc)).T.astype(BF16)], axis=0),
                g_end=jnp.exp(g_last)))
    for item, t_inv in zip(work, _unit_lower_inverses([item["low"] for item in work])):
        uw = _dot(t_inv.astype(BF16), item["vb_kbg16"])
        item["u"] = uw[:, :HEAD_DIM]
        item["w_qd16"] = jnp.concatenate([uw[:, HEAD_DIM:].astype(BF16), item["qd16"]], axis=0)

    for hh in range(hp):
        state = state_ref[hh]
        for item in work:
            if item["hh"] != hh:
                continue
            from_state = _dot(item["w_qd16"], state.astype(BF16))
            vn16 = (item["u"] - from_state[:c]).astype(BF16)
            from_new = _dot(item["qk_kdt16"], vn16)
            item["o"] = from_state[c:] + from_new[:c]
            state = state * item["g_end"] + from_new[c:]
        state_ref[hh] = state

    for item in work:
        rsl, cols = item["rsl"], item["cols"]
        z = z_ref[rsl, cols].astype(F32)
        o_ref[rsl, cols] = (_rms(item["o"]) * nw_ref[...] * _silu(z)).astype(o_ref.dtype)


def _delta(q, k, v, proj, z_col0, gates, gates_t, dn_norm, batch, seq, heads, w_src, layer):
    m, width = q.shape
    hp = min(16, heads)
    rows = min(CHUNK, seq)
    per_seq = seq // rows
    groups = heads // hp
    bw = hp * HEAD_DIM
    zb0 = z_col0 // bw
    _, w_rows, w_cols = w_src.shape
    w_block, w_blocks = _ride_along_blocks(w_rows, batch * groups * per_seq)
    w_idx = lambda b, h, t: jnp.minimum((b * groups + h) * per_seq + t, w_blocks - 1)
    qkv = pl.BlockSpec((rows, bw), lambda b, h, t: (b * per_seq + t, h))
    return pl.pallas_call(
        functools.partial(_delta_kernel, heads),
        grid=(batch, groups, per_seq),
        in_specs=[qkv, qkv, qkv,
                  pl.BlockSpec((rows, bw), lambda b, h, t: (b * per_seq + t, zb0 + h)),
                  pl.BlockSpec((rows, LANES), lambda b, h, t: (b * per_seq + t, 0)),
                  pl.BlockSpec((2 * heads, rows), lambda b, h, t: (0, b * per_seq + t)),
                  pl.BlockSpec((1, HEAD_DIM), lambda b, h, t: (0, 0)),
                  pl.BlockSpec((None, w_block, w_cols), lambda b, h, t: (layer, w_idx(b, h, t), 0))],
        out_specs=[qkv, pl.BlockSpec((w_block, w_cols), lambda b, h, t: (w_idx(b, h, t), 0))],
        out_shape=[jax.ShapeDtypeStruct((m, width), BF16), jax.ShapeDtypeStruct((w_rows, w_cols), BF16)],
        scratch_shapes=[pltpu.VMEM((hp, HEAD_DIM, HEAD_DIM), F32)],
        compiler_params=_params("arbitrary", "arbitrary", "arbitrary"),
        name="gated_delta",
    )(q, k, v, proj, gates, gates_t, dn_norm.reshape(1, HEAD_DIM), w_src)


def _merge_kernel(ga_ref, gb_ref, pa_ref, ob_ref, pp_ref, dp_ref, y_ref):
    ya = _sigmoid(ga_ref[...].astype(F32)) * _dot(pa_ref[...], pp_ref[...].astype(BF16))
    yb = _sigmoid(gb_ref[...].astype(F32)) * _dot(ob_ref[...], dp_ref[...].astype(BF16))
    y_ref[...] = (ya + yb).astype(y_ref.dtype)


def _merge(proj, gate_col0, pa, ob, pool_proj, dn_proj, layer):
    m, kp = pa.shape
    d = pool_proj.shape[2]
    tm, tn = min(1024, m), min(512, d)
    nb = d // tn
    assert gate_col0 % tn == 0
    g0 = gate_col0 // tn
    return pl.pallas_call(
        _merge_kernel,
        grid=(nb, m // tm),
        in_specs=[pl.BlockSpec((tm, tn), lambda j, i: (i, g0 + j)),
                  pl.BlockSpec((tm, tn), lambda j, i: (i, g0 + nb + j)),
                  pl.BlockSpec((tm, kp), lambda j, i: (i, 0)),
                  pl.BlockSpec((tm, kp), lambda j, i: (i, 0)),
                  pl.BlockSpec((None, kp, tn), lambda j, i: (layer, 0, j)),
                  pl.BlockSpec((None, kp, tn), lambda j, i: (layer, 0, j))],
        out_specs=pl.BlockSpec((tm, tn), lambda j, i: (i, j)),
        out_shape=jax.ShapeDtypeStruct((m, d), BF16),
        compiler_params=_params("parallel", "arbitrary"),
        name="gated_merge",
    )(proj, proj, pa, ob, pool_proj, dn_proj)


FFN_CARRY = 8


def _ffn_up_kernel(per_seq, h_ref, wa_ref, wu0_ref, wu1_ref, cw_ref, cb_ref, wsrc_ref, o_ref, w16_ref, a_ref):
    tm = h_ref.shape[0]
    w16_ref[...] = wsrc_ref[...].astype(BF16)

    @pl.when(pl.program_id(1) % per_seq == 0)
    def _():
        a_ref[0:FFN_CARRY, :] = jnp.zeros((FFN_CARRY, a_ref.shape[1]), F32)

    h = h_ref[...]
    a_ref[FFN_CARRY:, :] = _dot(h, wa_ref[...])
    u = jnp.concatenate([_dot(h, wu0_ref[...]), _dot(h, wu1_ref[...])], axis=1)
    a = cb_ref[...]
    for j in range(FFN_CONV_TAPS):
        r0 = FFN_CARRY - (FFN_CONV_TAPS - 1) + j
        a = a + a_ref[r0:r0 + tm, :] * cw_ref[j:j + 1, :]
    o_ref[...] = (_silu(a) * u).astype(o_ref.dtype)
    a_ref[0:FFN_CARRY, :] = a_ref[tm:tm + FFN_CARRY, :]


def _ffn_up(h, w_up, layer, conv_w, conv_b, seq, w_src):
    m, d = h.shape
    n = w_up.shape[1] // 2
    tm, tn = min(1024, seq), 512
    half = tn // 2
    assert n % half == 0
    nh = n // half
    nt = pl.cdiv(n, tn)
    per_seq = seq // tm
    pad = nt * tn - n
    conv_w = jnp.pad(conv_w, ((0, 0), (0, 0), (0, pad)))
    conv_b = jnp.pad(conv_b, ((0, 0), (0, pad))).reshape(conv_b.shape[0], 1, nt * tn)
    row_tiles = m // tm
    _, w_rows, w_cols = w_src.shape
    w_block, w_blocks = _ride_along_blocks(w_rows, nt * row_tiles)
    w_idx = lambda j, i: jnp.minimum(j * row_tiles + i, w_blocks - 1)
    return pl.pallas_call(
        functools.partial(_ffn_up_kernel, per_seq),
        grid=(nt, row_tiles),
        in_specs=[pl.BlockSpec((tm, d), lambda j, i: (i, 0)),
                  pl.BlockSpec((d, tn), lambda j, i: (0, j)),
                  pl.BlockSpec((d, half), lambda j, i: (0, nh + 2 * j)),
                  pl.BlockSpec((d, half), lambda j, i: (0, jnp.minimum(nh + 2 * j + 1, 2 * nh - 1))),
                  pl.BlockSpec((None, FFN_CONV_TAPS, tn), lambda j, i: (layer, 0, j)),
                  pl.BlockSpec((None, 1, tn), lambda j, i: (layer, 0, j)),
                  pl.BlockSpec((None, w_block, w_cols), lambda j, i: (layer, w_idx(j, i), 0))],
        out_specs=[pl.BlockSpec((tm, tn), lambda j, i: (i, j)),
                   pl.BlockSpec((w_block, w_cols), lambda j, i: (w_idx(j, i), 0))],
        out_shape=[jax.ShapeDtypeStruct((m, n), BF16), jax.ShapeDtypeStruct((w_rows, w_cols), BF16)],
        scratch_shapes=[pltpu.VMEM((tm + FFN_CARRY, tn), F32)],
        compiler_params=_params("arbitrary", "arbitrary"),
        name="ffn_up_conv_gate",
    )(h, w_up, w_up, w_up, conv_w, conv_b, w_src)


def kernel(x, c, w_ada, b_ada, ada_table, mix_pre_norm, w_in, pool_w, pool_scale, pool_proj,
           dn_conv, dn_a_log, dn_dt_bias, dn_norm, dn_proj, w_o, mix_post_norm,
           ffn_pre_norm, w_up, ffn_conv, ffn_conv_b, w_down, ffn_post_norm):
    batch, seq, d = x.shape
    depth = w_in.shape[0]
    m = batch * seq
    pool_width = pool_scale.shape[1]
    dn_width = dn_proj.shape[1]
    heads = dn_a_log.shape[1]
    c_qkv = pool_width
    c_z = c_qkv + 3 * dn_width
    c_ba = c_z + dn_width

    c8 = jnp.pad(c, ((0, 8 - batch), (0, 0)))
    mod = _ada(c8, w_ada, b_ada, ada_table.reshape(depth, N_MOD * d))
    mod = mod[:, :batch].reshape(depth, batch, N_MOD, 1, d)

    xf = x.reshape(m, d)
    h = _norm_mod(xf, mix_pre_norm[0], mod[0, :, 1], mod[0, :, 0], seq)
    w_in_t = jnp.swapaxes(w_in, 1, 2)
    for l in range(depth):
        shift_f, scale_f, gate_m, gate_f = mod[l, :, 3], mod[l, :, 4], mod[l, :, 2], mod[l, :, 5]
        proj = _matmul_nt(h, w_in_t, l, c_ba + 2 * d, c_ba, 2 * heads, BF16, 1024, 512, "in_proj")
        gates = _ba(h, w_in_t, l, c_ba, dn_a_log[l], dn_dt_bias[l], heads)
        gates_t = gates[:, :2 * heads].T
        pa = _pool(proj, pool_w, l, pool_scale[l], seq)
        q, k, v = _qkv_conv(proj, dn_conv[l], dn_width, c_qkv, seq)
        ob, w_up16 = _delta(q, k, v, proj, c_z, gates, gates_t, dn_norm[l], batch, seq, heads, w_up, l)
        y = _merge(proj, c_ba, pa, ob, pool_proj, dn_proj, l)
        yo = _matmul(y, w_o, l, d, BF16, 1024, 512, "out_proj", rows_outer=True)
        xf, h = _resid_norm(yo, xf, mix_post_norm[l], gate_m, seq,
                            nxt=(ffn_pre_norm[l], scale_f, shift_f))
        act, w_down16 = _ffn_up(h, w_up16, l, ffn_conv, ffn_conv_b, seq, w_down)
        yd = _matmul(act, w_down16[None], 0, d, BF16, 512, 512, "ffn_down")
        if l + 1 < depth:
            xf, h = _resid_norm(yd, xf, ffn_post_norm[l], gate_f, seq,
                                nxt=(mix_pre_norm[l + 1], mod[l + 1, :, 1], mod[l + 1, :, 0]))
        else:
            xf = _resid_norm(yd, xf, ffn_post_norm[l], gate_f, seq)
    return xf.reshape(batch, seq, d)
```

```python
import functools

import jax
import jax.numpy as jnp
from jax import lax
from jax.experimental import pallas as pl
from jax.experimental.pallas import tpu as pltpu

F32 = jnp.float32
BF16 = jnp.bfloat16

EPS = 1e-6
POOL_WINDOWS = (2, 4, 8, 16)
HEAD_DIM = 128
DN_CONV_TAPS = 4
FFN_CONV_TAPS = 3
N_MOD = 6
CHUNK = 128
HALO = 16
VMEM_LIMIT_BYTES = 56 * 1024 * 1024
LANES = 128


def _params(*semantics):
    return pltpu.CompilerParams(dimension_semantics=semantics,
                                vmem_limit_bytes=VMEM_LIMIT_BYTES)


def _dot(a, b):
    return jnp.dot(a, b, preferred_element_type=F32)


def _dot_nt(a, b):
    return lax.dot_general(a, b, (((1,), (1,)), ((), ())), preferred_element_type=F32)


def _sigmoid(x):
    return 1.0 / (1.0 + jnp.exp(-x))


def _silu(x):
    return x * _sigmoid(x)


def _softplus(x):
    return jnp.maximum(x, 0.0) + jnp.log1p(jnp.exp(-jnp.abs(x)))


def _rms(x):
    return x * lax.rsqrt(jnp.mean(x * x, axis=-1, keepdims=True) + EPS)


def _ada_kernel(c_ref, w_ref, b_ref, t_ref, o_ref):
    a = _silu(c_ref[...]).astype(BF16)
    acc = _dot(a, w_ref[...].astype(BF16)) + b_ref[...]
    o_ref[...] = acc[None] + t_ref[...]


def _ada(c8, w_ada, b_ada, table):
    depth = table.shape[0]
    d, n = w_ada.shape
    tn = min(512, n)
    return pl.pallas_call(
        _ada_kernel,
        grid=(n // tn,),
        in_specs=[pl.BlockSpec((8, d), lambda j: (0, 0)),
                  pl.BlockSpec((d, tn), lambda j: (0, j)),
                  pl.BlockSpec((1, tn), lambda j: (0, j)),
                  pl.BlockSpec((depth, 1, tn), lambda j: (0, 0, j))],
        out_specs=pl.BlockSpec((depth, 8, tn), lambda j: (0, 0, j)),
        out_shape=jax.ShapeDtypeStruct((depth, 8, n), F32),
        compiler_params=_params("arbitrary"),
        name="ada",
    )(c8, w_ada, b_ada.reshape(1, n), table.reshape(depth, 1, n))


def _norm_mod_kernel(x_ref, w_ref, sc_ref, sh_ref, h_ref):
    y = _rms(x_ref[...]) * w_ref[...]
    h_ref[...] = (y * (1.0 + sc_ref[0]) + sh_ref[0]).astype(h_ref.dtype)


def _norm_mod(x, w, scale, shift, seq):
    m, d = x.shape
    tm = min(256, seq)
    per_seq = seq // tm
    row = pl.BlockSpec((tm, d), lambda i: (i, 0))
    vec = pl.BlockSpec((1, d), lambda i: (0, 0))
    bvec = pl.BlockSpec((1, 1, d), lambda i: (i // per_seq, 0, 0))
    return pl.pallas_call(
        _norm_mod_kernel,
        grid=(m // tm,),
        in_specs=[row, vec, bvec, bvec],
        out_specs=row,
        out_shape=jax.ShapeDtypeStruct((m, d), BF16),
        compiler_params=_params("parallel"),
        name="norm_mod",
    )(x, w.reshape(1, d), scale, shift)


def _resid_norm_kernel(y_ref, x_ref, pw_ref, g_ref, nw_ref, sc_ref, sh_ref, xo_ref, h_ref):
    xn = x_ref[...] + g_ref[0] * (_rms(y_ref[...].astype(F32)) * pw_ref[...])
    xo_ref[...] = xn
    h = _rms(xn) * nw_ref[...]
    h_ref[...] = (h * (1.0 + sc_ref[0]) + sh_ref[0]).astype(h_ref.dtype)


def _resid_kernel(y_ref, x_ref, pw_ref, g_ref, xo_ref):
    xo_ref[...] = x_ref[...] + g_ref[0] * (_rms(y_ref[...].astype(F32)) * pw_ref[...])


def _resid_norm(y, x, post_w, gate, seq, nxt=None):
    m, d = x.shape
    tm = min(256, seq)
    per_seq = seq // tm
    row = pl.BlockSpec((tm, d), lambda i: (i, 0))
    vec = pl.BlockSpec((1, d), lambda i: (0, 0))
    bvec = pl.BlockSpec((1, 1, d), lambda i: (i // per_seq, 0, 0))
    if nxt is None:
        return pl.pallas_call(
            _resid_kernel,
            grid=(m // tm,),
            in_specs=[row, row, vec, bvec],
            out_specs=row,
            out_shape=jax.ShapeDtypeStruct((m, d), F32),
            compiler_params=_params("parallel"),
            name="resid",
        )(y, x, post_w.reshape(1, d), gate)
    norm_w, scale, shift = nxt
    return pl.pallas_call(
        _resid_norm_kernel,
        grid=(m // tm,),
        in_specs=[row, row, vec, bvec, vec, bvec, bvec],
        out_specs=[row, row],
        out_shape=[jax.ShapeDtypeStruct((m, d), F32), jax.ShapeDtypeStruct((m, d), BF16)],
        compiler_params=_params("parallel"),
        name="resid_norm",
    )(y, x, post_w.reshape(1, d), gate, norm_w.reshape(1, d), scale, shift)


def _mm_kernel(a_ref, w_ref, o_ref):
    o_ref[...] = _dot(a_ref[...], w_ref[...].astype(BF16)).astype(o_ref.dtype)


def _matmul(a, w, layer, n, out_dtype, tm, tn, name, rows_outer=False):
    m, k = a.shape
    tm, tn = min(tm, m), min(tn, n)
    if rows_outer:
        grid, ij, a_mode = (m // tm, n // tn), (lambda i, j: (i, j)), pl.Buffered(1)
    else:
        grid, ij, a_mode = (n // tn, m // tm), (lambda j, i: (i, j)), None
    return pl.pallas_call(
        _mm_kernel,
        grid=grid,
        in_specs=[pl.BlockSpec((tm, k), lambda *g: (ij(*g)[0], 0), pipeline_mode=a_mode),
                  pl.BlockSpec((None, k, tn), lambda *g: (layer, 0, ij(*g)[1]))],
        out_specs=pl.BlockSpec((tm, tn), lambda *g: ij(*g)),
        out_shape=jax.ShapeDtypeStruct((m, n), out_dtype),
        compiler_params=_params("parallel", "arbitrary"),
        name=name,
    )(a, w)


def _mm_nt_kernel(a_ref, wt_ref, o_ref):
    o_ref[...] = _dot_nt(a_ref[...], wt_ref[0].astype(BF16)).astype(o_ref.dtype)


def _matmul_nt(a, wt, layer, n, skip_at, skip, out_dtype, tm, tn, name):
    m, k = a.shape
    tm, tn = min(tm, m), min(tn, n)
    sublanes = 8 * 4 // wt.dtype.itemsize
    assert skip % sublanes == 0 and n % tn == 0 and skip_at % tn == 0
    first_after = skip_at // tn
    row = lambda j: pl.multiple_of(j * tn + jnp.where(j >= first_after, skip, 0), sublanes)
    return pl.pallas_call(
        _mm_nt_kernel,
        grid=(m // tm, n // tn),
        in_specs=[pl.BlockSpec((tm, k), lambda i, j: (i, 0), pipeline_mode=pl.Buffered(1)),
                  pl.BlockSpec((pl.Element(1), pl.Element(tn), pl.Element(k)),
                               lambda i, j: (layer, row(j), 0))],
        out_specs=pl.BlockSpec((tm, tn), lambda i, j: (i, j)),
        out_shape=jax.ShapeDtypeStruct((m, n), out_dtype),
        compiler_params=_params("parallel", "arbitrary"),
        name=name,
    )(a, wt)


def _ba_kernel(heads, h_ref, w_ref, alog_ref, dtb_ref, o_ref):
    p = _dot_nt(h_ref[...], w_ref[...].astype(BF16))
    lane = lax.broadcasted_iota(jnp.int32, p.shape, 1)
    is_g = (lane >= heads) & (lane < 2 * heads)
    g = jnp.where(is_g, -jnp.exp(alog_ref[...]) * _softplus(p + dtb_ref[...]), 0.0)
    pos = lax.broadcasted_iota(jnp.int32, p.shape, 0) % CHUNK
    shift = 1
    while shift < CHUNK:
        g = g + jnp.where(pos >= shift, pltpu.roll(g, shift, axis=0), 0.0)
        shift *= 2
    o_ref[...] = jnp.where(is_g, g, _sigmoid(p))


def _ba(h, wt, layer, row0, a_log, dt_bias, heads):
    m, k = h.shape
    tm = min(512, m)
    assert row0 % LANES == 0 and 2 * heads <= LANES
    pad = lambda v: jnp.zeros((1, LANES), F32).at[0, heads:2 * heads].set(v)
    return pl.pallas_call(
        functools.partial(_ba_kernel, heads),
        grid=(m // tm,),
        in_specs=[pl.BlockSpec((tm, k), lambda i: (i, 0)),
                  pl.BlockSpec((None, LANES, k), lambda i: (layer, row0 // LANES, 0)),
                  pl.BlockSpec((1, LANES), lambda i: (0, 0)),
                  pl.BlockSpec((1, LANES), lambda i: (0, 0))],
        out_specs=pl.BlockSpec((tm, LANES), lambda i: (i, 0)),
        out_shape=jax.ShapeDtypeStruct((m, LANES), F32),
        compiler_params=_params("parallel"),
        name="dn_gates",
    )(h, wt, pad(a_log), pad(dt_bias))


def _band(tm, lo, hi):
    ri = lax.broadcasted_iota(jnp.int32, (tm, tm + HALO), 0)
    ci = lax.broadcasted_iota(jnp.int32, (tm, tm + HALO), 1)
    back = ri + HALO - ci
    return jnp.where((back >= lo) & (back <= hi), 1.0, 0.0).astype(BF16)


def _pool_kernel(per_seq, u_ref, halo_ref, pw_ref, sc_ref, o_ref, ext_ref):
    tm, width = u_ref.shape
    group = width // len(POOL_WINDOWS)
    t = pl.program_id(0) % per_seq
    ext_ref[0:HALO, :] = jnp.where(t == 0, jnp.zeros(halo_ref.shape, halo_ref.dtype), halo_ref[...])
    ext_ref[HALO:, :] = u_ref[...]
    pos = t * tm + lax.broadcasted_iota(jnp.int32, (tm, 1), 0)
    for gi, win in enumerate(POOL_WINDOWS):
        cols = slice(gi * group, (gi + 1) * group)
        window_sum = _dot(_band(tm, 0, win - 1), ext_ref[:, cols])
        inv_count = 1.0 / jnp.minimum(pos + 1, win).astype(F32)
        pa = window_sum * inv_count - u_ref[:, cols].astype(F32)
        r = _dot(pa.astype(BF16), pw_ref[gi].astype(BF16)) * sc_ref[:, cols]
        o_ref[:, cols] = r.astype(o_ref.dtype)


def _pool(proj, pool_w, layer, pool_scale, seq):
    m = proj.shape[0]
    _, groups, group, _ = pool_w.shape
    width = groups * group
    tm = min(256, seq)
    per_seq = seq // tm
    return pl.pallas_call(
        functools.partial(_pool_kernel, per_seq),
        grid=(m // tm,),
        in_specs=[pl.BlockSpec((tm, width), lambda i: (i, 0)),
                  pl.BlockSpec((HALO, width), lambda i: (jnp.maximum(i * (tm // HALO) - 1, 0), 0)),
                  pl.BlockSpec((None, groups, group, group), lambda i: (layer, 0, 0, 0)),
                  pl.BlockSpec((1, width), lambda i: (0, 0))],
        out_specs=pl.BlockSpec((tm, width), lambda i: (i, 0)),
        out_shape=jax.ShapeDtypeStruct((m, width), BF16),
        scratch_shapes=[pltpu.VMEM((tm + HALO, width), BF16)],
        compiler_params=_params("parallel"),
        name="pool_mixer",
    )(proj, proj, pool_w, pool_scale.reshape(1, width))


def _qkv_kernel(per_seq, q_ref, k_ref, v_ref, hq_ref, hk_ref, hv_ref, wq_ref, wk_ref, wv_ref,
                qo_ref, ko_ref, vo_ref, ext_ref):
    tm, width = q_ref.shape
    first = pl.program_id(0) % per_seq == 0
    taps = DN_CONV_TAPS
    shifts = jnp.concatenate([_band(tm, s, s) for s in range(1, taps)], axis=0)
    chunk = 2 * HEAD_DIM
    for x_ref, halo_ref, w_ref, o_ref, norm in (
            (q_ref, hq_ref, wq_ref, qo_ref, HEAD_DIM ** -0.5),
            (k_ref, hk_ref, wk_ref, ko_ref, 1.0),
            (v_ref, hv_ref, wv_ref, vo_ref, None)):
        ext_ref[0:HALO, :] = jnp.where(first, jnp.zeros(halo_ref.shape, halo_ref.dtype), halo_ref[...])
        ext_ref[HALO:, :] = x_ref[...]
        for c0 in range(0, width, chunk):
            wide = slice(c0, c0 + chunk)
            back = _dot(shifts, ext_ref[:, wide])
            y = x_ref[:, wide].astype(F32) * w_ref[taps - 1:taps, wide]
            for s in range(1, taps):
                y = y + back[(s - 1) * tm:s * tm] * w_ref[taps - 1 - s:taps - s, wide]
            y = _silu(y)
            for c in range(0, chunk, HEAD_DIM):
                yh = y[:, c:c + HEAD_DIM]
                if norm is not None:
                    yh = yh * (lax.rsqrt(jnp.sum(yh * yh, axis=-1, keepdims=True) + EPS) * norm)
                o_ref[:, c0 + c:c0 + c + HEAD_DIM] = yh.astype(o_ref.dtype)


def _qkv_conv(proj, dn_conv, width, col0, seq):
    m = proj.shape[0]
    tm = min(128, seq)
    per_seq = seq // tm
    b0 = col0 // width
    rows = [pl.BlockSpec((tm, width), functools.partial(lambda s, i: (i, b0 + s), s)) for s in range(3)]
    halos = [pl.BlockSpec((HALO, width),
                          functools.partial(lambda s, i: (jnp.maximum(i * (tm // HALO) - 1, 0), b0 + s), s))
             for s in range(3)]
    taps = [pl.BlockSpec((DN_CONV_TAPS, width), functools.partial(lambda s, i: (0, s), s)) for s in range(3)]
    out = pl.BlockSpec((tm, width), lambda i: (i, 0))
    shp = jax.ShapeDtypeStruct((m, width), BF16)
    return pl.pallas_call(
        functools.partial(_qkv_kernel, per_seq),
        grid=(m // tm,),
        in_specs=rows + halos + taps,
        out_specs=[out, out, out],
        out_shape=[shp, shp, shp],
        scratch_shapes=[pltpu.VMEM((tm + HALO, width), BF16)],
        compiler_params=_params("parallel"),
        name="dn_qkv_conv",
    )(proj, proj, proj, proj, proj, proj, dn_conv, dn_conv, dn_conv)


def _unit_lower_inverses(lows):
    n = lows[0].shape[0]
    ri = lax.broadcasted_iota(jnp.int32, (n, n), 0)
    ci = lax.broadcasted_iota(jnp.int32, (n, n), 1)
    same = lambda bits: jnp.right_shift(ri, bits) == jnp.right_shift(ci, bits)
    eye = jnp.where(ri == ci, 1.0, 0.0)
    invs = [eye - jnp.where(same(1), low, 0.0) for low in lows]
    bits = 1
    while (1 << bits) < n:
        mask = same(bits + 1) & jnp.logical_not(same(bits))
        inv16 = [inv.astype(BF16) for inv in invs]
        right = [_dot(jnp.where(mask, low, 0.0).astype(BF16), i16) for low, i16 in zip(lows, inv16)]
        invs = [inv - _dot(i16, r.astype(BF16)) for inv, i16, r in zip(invs, inv16, right)]
        bits += 1
    return invs


def _ride_along_blocks(rows, steps):
    block = next(b for b in range(16, rows + 1, 16) if rows % b == 0 and rows // b <= steps)
    return block, rows // block


def _delta_kernel(heads, q_ref, k_ref, v_ref, z_ref, g_ref, gt_ref, nw_ref, wsrc_ref, o_ref, w16_ref,
                  state_ref):
    rows, width = q_ref.shape
    hp = width // HEAD_DIM
    c = CHUNK
    w16_ref[...] = wsrc_ref[...].astype(BF16)

    @pl.when(pl.program_id(2) == 0)
    def _():
        state_ref[...] = jnp.zeros_like(state_ref)

    gates = g_ref[...]
    lane = lax.broadcasted_iota(jnp.int32, gates.shape, 1)
    ri = lax.broadcasted_iota(jnp.int32, (c, c), 0)
    ci = lax.broadcasted_iota(jnp.int32, (c, c), 1)
    causal = ri >= ci
    strict = ri > ci

    work = []
    for hh in range(hp):
        head = pl.program_id(1) * hp + hh
        cols = slice(hh * HEAD_DIM, (hh + 1) * HEAD_DIM)
        beta_col = jnp.sum(jnp.where(lane == head, gates, 0.0), axis=-1, keepdims=True)
        gc_col = jnp.sum(jnp.where(lane == heads + head, gates, 0.0), axis=-1, keepdims=True)
        gc_row = gt_ref[pl.ds(heads + head, 1), :]
        for r0 in range(0, rows, c):
            rsl = slice(r0, r0 + c)
            q = q_ref[rsl, cols]
            k16 = k_ref[rsl, cols]
            k = k16.astype(F32)
            beta = beta_col[rsl]
            gcc = gc_col[rsl]
            gcr = gc_row[:, rsl]
            g_last = gcr[:, c - 1:c]
            decay = jnp.exp(jnp.where(causal, gcc - gcr, -jnp.inf))
            e_g = jnp.exp(gcc)
            kb = k * beta
            kq_k = _dot_nt(jnp.concatenate([kb.astype(BF16), q], axis=0), k16)
            low = jnp.where(strict, kq_k[:c] * decay, 0.0)
            qk = jnp.where(causal, kq_k[c:] * decay, 0.0)
            work.append(dict(
                hh=hh, rsl=rsl, cols=cols, low=low,
                vb_kbg16=jnp.concatenate([(v_ref[rsl, cols].astype(F32) * beta).astype(BF16),
                                          (kb * e_g).astype(BF16)], axis=1),
                qd16=(q.astype(F32) * e_g).astype(BF16),
                qk_kdt16=jnp.concatenate([qk.astype(BF16),
                                          (k * jnp.exp(g_last - gcc)).T.astype(BF16)], axis=0),
                g_end=jnp.exp(g_last)))
    for item, t_inv in zip(work, _unit_lower_inverses([item["low"] for item in work])):
        uw = _dot(t_inv.astype(BF16), item["vb_kbg16"])
        item["u"] = uw[:, :HEAD_DIM]
        item["w_qd16"] = jnp.concatenate([uw[:, HEAD_DIM:].astype(BF16), item["qd16"]], axis=0)

    for hh in range(hp):
        state = state_ref[hh]
        for item in work:
            if item["hh"] != hh:
                continue
            from_state = _dot(item["w_qd16"], state.astype(BF16))
            vn16 = (item["u"] - from_state[:c]).astype(BF16)
            from_new = _dot(item["qk_kdt16"], vn16)
            item["o"] = from_state[c:] + from_new[:c]
            state = state * item["g_end"] + from_new[c:]
        state_ref[hh] = state

    for item in work:
        rsl, cols = item["rsl"], item["cols"]
        z = z_ref[rsl, cols].astype(F32)
        o_ref[rsl, cols] = (_rms(item["o"]) * nw_ref[...] * _silu(z)).astype(o_ref.dtype)


def _delta(q, k, v, proj, z_col0, gates, gates_t, dn_norm, batch, seq, heads, w_src, layer):
    m, width = q.shape
    hp = min(16, heads)
    rows = min(CHUNK, seq)
    per_seq = seq // rows
    groups = heads // hp
    bw = hp * HEAD_DIM
    zb0 = z_col0 // bw
    _, w_rows, w_cols = w_src.shape
    w_block, w_blocks = _ride_along_blocks(w_rows, batch * groups * per_seq)
    w_idx = lambda b, h, t: jnp.minimum((b * groups + h) * per_seq + t, w_blocks - 1)
    qkv = pl.BlockSpec((rows, bw), lambda b, h, t: (b * per_seq + t, h))
    return pl.pallas_call(
        functools.partial(_delta_kernel, heads),
        grid=(batch, groups, per_seq),
        in_specs=[qkv, qkv, qkv,
                  pl.BlockSpec((rows, bw), lambda b, h, t: (b * per_seq + t, zb0 + h)),
                  pl.BlockSpec((rows, LANES), lambda b, h, t: (b * per_seq + t, 0)),
                  pl.BlockSpec((2 * heads, rows), lambda b, h, t: (0, b * per_seq + t)),
                  pl.BlockSpec((1, HEAD_DIM), lambda b, h, t: (0, 0)),
                  pl.BlockSpec((None, w_block, w_cols), lambda b, h, t: (layer, w_idx(b, h, t), 0))],
        out_specs=[qkv, pl.BlockSpec((w_block, w_cols), lambda b, h, t: (w_idx(b, h, t), 0))],
        out_shape=[jax.ShapeDtypeStruct((m, width), BF16), jax.ShapeDtypeStruct((w_rows, w_cols), BF16)],
        scratch_shapes=[pltpu.VMEM((hp, HEAD_DIM, HEAD_DIM), F32)],
        compiler_params=_params("arbitrary", "arbitrary", "arbitrary"),
        name="gated_delta",
    )(q, k, v, proj, gates, gates_t, dn_norm.reshape(1, HEAD_DIM), w_src)


def _merge_kernel(ga_ref, gb_ref, pa_ref, ob_ref, pp_ref, dp_ref, y_ref):
    ya = _sigmoid(ga_ref[...].astype(F32)) * _dot(pa_ref[...], pp_ref[...].astype(BF16))
    yb = _sigmoid(gb_ref[...].astype(F32)) * _dot(ob_ref[...], dp_ref[...].astype(BF16))
    y_ref[...] = (ya + yb).astype(y_ref.dtype)


def _merge(proj, gate_col0, pa, ob, pool_proj, dn_proj, layer):
    m, kp = pa.shape
    d = pool_proj.shape[2]
    tm, tn = min(1024, m), min(512, d)
    nb = d // tn
    assert gate_col0 % tn == 0
    g0 = gate_col0 // tn
    resident = pl.Buffered(1)
    return pl.pallas_call(
        _merge_kernel,
        grid=(m // tm, nb),
        in_specs=[pl.BlockSpec((tm, tn), lambda i, j: (i, g0 + j)),
                  pl.BlockSpec((tm, tn), lambda i, j: (i, g0 + nb + j)),
                  pl.BlockSpec((tm, kp), lambda i, j: (i, 0), pipeline_mode=resident),
                  pl.BlockSpec((tm, kp), lambda i, j: (i, 0), pipeline_mode=resident),
                  pl.BlockSpec((None, kp, tn), lambda i, j: (layer, 0, j)),
                  pl.BlockSpec((None, kp, tn), lambda i, j: (layer, 0, j))],
        out_specs=pl.BlockSpec((tm, tn), lambda i, j: (i, j)),
        out_shape=jax.ShapeDtypeStruct((m, d), BF16),
        compiler_params=_params("parallel", "arbitrary"),
        name="gated_merge",
    )(proj, proj, pa, ob, pool_proj, dn_proj)


FFN_CARRY = 8


def _ffn_up_kernel(per_seq, h_ref, wa_ref, wu0_ref, wu1_ref, cw_ref, cb_ref, wsrc_ref, o_ref, w16_ref,
                   a_ref, carry_ref):
    tm = h_ref.shape[0]
    j = pl.program_id(1)
    w16_ref[...] = wsrc_ref[...].astype(BF16)
    starts_sequence = pl.program_id(0) % per_seq == 0

    @pl.when(starts_sequence)
    def _():
        a_ref[0:FFN_CARRY, :] = jnp.zeros((FFN_CARRY, a_ref.shape[1]), F32)

    @pl.when(jnp.logical_not(starts_sequence))
    def _():
        a_ref[0:FFN_CARRY, :] = carry_ref[j]

    h = h_ref[...]
    a_ref[FFN_CARRY:, :] = _dot(h, wa_ref[...])
    u = jnp.concatenate([_dot(h, wu0_ref[...]), _dot(h, wu1_ref[...])], axis=1)
    a = cb_ref[...]
    for tap in range(FFN_CONV_TAPS):
        r0 = FFN_CARRY - (FFN_CONV_TAPS - 1) + tap
        a = a + a_ref[r0:r0 + tm, :] * cw_ref[tap:tap + 1, :]
    o_ref[...] = (_silu(a) * u).astype(o_ref.dtype)
    carry_ref[j] = a_ref[tm:tm + FFN_CARRY, :]


def _ffn_up(h, w_up, layer, conv_w, conv_b, seq, w_src):
    m, d = h.shape
    n = w_up.shape[1] // 2
    tm, tn = min(1024, seq), 512
    half = tn // 2
    assert n % half == 0
    nh = n // half
    nt = pl.cdiv(n, tn)
    per_seq = seq // tm
    pad = nt * tn - n
    conv_w = jnp.pad(conv_w, ((0, 0), (0, 0), (0, pad)))
    conv_b = jnp.pad(conv_b, ((0, 0), (0, pad))).reshape(conv_b.shape[0], 1, nt * tn)
    row_tiles = m // tm
    _, w_rows, w_cols = w_src.shape
    w_block, w_blocks = _ride_along_blocks(w_rows, nt * row_tiles)
    w_idx = lambda i, j: jnp.minimum(i * nt + j, w_blocks - 1)
    return pl.pallas_call(
        functools.partial(_ffn_up_kernel, per_seq),
        grid=(row_tiles, nt),
        in_specs=[pl.BlockSpec((tm, d), lambda i, j: (i, 0), pipeline_mode=pl.Buffered(1)),
                  pl.BlockSpec((d, tn), lambda i, j: (0, j)),
                  pl.BlockSpec((d, half), lambda i, j: (0, nh + 2 * j)),
                  pl.BlockSpec((d, half), lambda i, j: (0, jnp.minimum(nh + 2 * j + 1, 2 * nh - 1))),
                  pl.BlockSpec((None, FFN_CONV_TAPS, tn), lambda i, j: (layer, 0, j)),
                  pl.BlockSpec((None, 1, tn), lambda i, j: (layer, 0, j)),
                  pl.BlockSpec((None, w_block, w_cols), lambda i, j: (layer, w_idx(i, j), 0))],
        out_specs=[pl.BlockSpec((tm, tn), lambda i, j: (i, j)),
                   pl.BlockSpec((w_block, w_cols), lambda i, j: (w_idx(i, j), 0))],
        out_shape=[jax.ShapeDtypeStruct((m, n), BF16), jax.ShapeDtypeStruct((w_rows, w_cols), BF16)],
        scratch_shapes=[pltpu.VMEM((tm + FFN_CARRY, tn), F32), pltpu.VMEM((nt, FFN_CARRY, tn), F32)],
        compiler_params=_params("arbitrary", "arbitrary"),
        name="ffn_up_conv_gate",
    )(h, w_up, w_up, w_up, conv_w, conv_b, w_src)


def kernel(x, c, w_ada, b_ada, ada_table, mix_pre_norm, w_in, pool_w, pool_scale, pool_proj,
           dn_conv, dn_a_log, dn_dt_bias, dn_norm, dn_proj, w_o, mix_post_norm,
           ffn_pre_norm, w_up, ffn_conv, ffn_conv_b, w_down, ffn_post_norm):
    batch, seq, d = x.shape
    depth = w_in.shape[0]
    m = batch * seq
    pool_width = pool_scale.shape[1]
    dn_width = dn_proj.shape[1]
    heads = dn_a_log.shape[1]
    c_qkv = pool_width
    c_z = c_qkv + 3 * dn_width
    c_ba = c_z + dn_width

    c8 = jnp.pad(c, ((0, 8 - batch), (0, 0)))
    mod = _ada(c8, w_ada, b_ada, ada_table.reshape(depth, N_MOD * d))
    mod = mod[:, :batch].reshape(depth, batch, N_MOD, 1, d)

    xf = x.reshape(m, d)
    h = _norm_mod(xf, mix_pre_norm[0], mod[0, :, 1], mod[0, :, 0], seq)
    w_in_t = jnp.swapaxes(w_in, 1, 2)
    for l in range(depth):
        shift_f, scale_f, gate_m, gate_f = mod[l, :, 3], mod[l, :, 4], mod[l, :, 2], mod[l, :, 5]
        proj = _matmul_nt(h, w_in_t, l, c_ba + 2 * d, c_ba, 2 * heads, BF16, 2048, 512, "in_proj")
        gates = _ba(h, w_in_t, l, c_ba, dn_a_log[l], dn_dt_bias[l], heads)
        gates_t = gates[:, :2 * heads].T
        pa = _pool(proj, pool_w, l, pool_scale[l], seq)
        q, k, v = _qkv_conv(proj, dn_conv[l], dn_width, c_qkv, seq)
        ob, w_up16 = _delta(q, k, v, proj, c_z, gates, gates_t, dn_norm[l], batch, seq, heads, w_up, l)
        y = _merge(proj, c_ba, pa, ob, pool_proj, dn_proj, l)
        yo = _matmul(y, w_o, l, d, BF16, 2048, 512, "out_proj", rows_outer=True)
        xf, h = _resid_norm(yo, xf, mix_post_norm[l], gate_m, seq,
                            nxt=(ffn_pre_norm[l], scale_f, shift_f))
        act, w_down16 = _ffn_up(h, w_up16, l, ffn_conv, ffn_conv_b, seq, w_down)
        yd = _matmul(act, w_down16[None], 0, d, BF16, 512, 512, "ffn_down")
        if l + 1 < depth:
            xf, h = _resid_norm(yd, xf, ffn_post_norm[l], gate_f, seq,
                                nxt=(mix_pre_norm[l + 1], mod[l + 1, :, 1], mod[l + 1, :, 0]))
        else:
            xf = _resid_norm(yd, xf, ffn_post_norm[l], gate_f, seq)
    return xf.reshape(batch, seq, d)
```

```python
import functools

import jax
import jax.numpy as jnp
from jax import lax
from jax.experimental import pallas as pl
from jax.experimental.pallas import tpu as pltpu

F32 = jnp.float32
BF16 = jnp.bfloat16

EPS = 1e-6
POOL_WINDOWS = (2, 4, 8, 16)
HEAD_DIM = 128
DN_CONV_TAPS = 4
FFN_CONV_TAPS = 3
N_MOD = 6
CHUNK = 128
HALO = 16
VMEM_LIMIT_BYTES = 56 * 1024 * 1024
LANES = 128
SUBLANES = 8

ELEMENTWISE_ROWS = 256
CONV_ROWS = 128
GATE_ROWS = 512
MATMUL_ROWS, MATMUL_COLS = 1024, 512
IN_PROJ_ROWS = 2048
FFN_DOWN_ROWS = 512
DELTA_HEADS_PER_STEP = 16
ADA_COLS = 512


def _params(*semantics):
    return pltpu.CompilerParams(dimension_semantics=semantics,
                                vmem_limit_bytes=VMEM_LIMIT_BYTES)


def _dot(a, b):
    return jnp.dot(a, b, preferred_element_type=F32)


def _dot_nt(a, b):
    return lax.dot_general(a, b, (((1,), (1,)), ((), ())), preferred_element_type=F32)


def _sigmoid(x):
    return 1.0 / (1.0 + jnp.exp(-x))


def _silu(x):
    return x * _sigmoid(x)


def _softplus(x):
    return jnp.maximum(x, 0.0) + jnp.log1p(jnp.exp(-jnp.abs(x)))


def _rms(x):
    return x * lax.rsqrt(jnp.mean(x * x, axis=-1, keepdims=True) + EPS)


def _ada_kernel(c_ref, w_ref, b_ref, t_ref, o_ref):
    a = _silu(c_ref[...]).astype(BF16)
    acc = _dot(a, w_ref[...].astype(BF16)) + b_ref[...]
    o_ref[...] = acc[None] + t_ref[...]


def _ada(c8, w_ada, b_ada, table):
    depth = table.shape[0]
    d, n = w_ada.shape
    tn = min(ADA_COLS, n)
    return pl.pallas_call(
        _ada_kernel,
        grid=(n // tn,),
        in_specs=[pl.BlockSpec((SUBLANES, d), lambda j: (0, 0)),
                  pl.BlockSpec((d, tn), lambda j: (0, j)),
                  pl.BlockSpec((1, tn), lambda j: (0, j)),
                  pl.BlockSpec((depth, 1, tn), lambda j: (0, 0, j))],
        out_specs=pl.BlockSpec((depth, SUBLANES, tn), lambda j: (0, 0, j)),
        out_shape=jax.ShapeDtypeStruct((depth, SUBLANES, n), F32),
        compiler_params=_params("arbitrary"),
        name="ada",
    )(c8, w_ada, b_ada.reshape(1, n), table.reshape(depth, 1, n))


def _norm_mod_kernel(x_ref, w_ref, sc_ref, sh_ref, h_ref):
    y = _rms(x_ref[...]) * w_ref[...]
    h_ref[...] = (y * (1.0 + sc_ref[0]) + sh_ref[0]).astype(h_ref.dtype)


def _norm_mod(x, w, scale, shift, seq):
    m, d = x.shape
    tm = min(ELEMENTWISE_ROWS, seq)
    per_seq = seq // tm
    row = pl.BlockSpec((tm, d), lambda i: (i, 0))
    vec = pl.BlockSpec((1, d), lambda i: (0, 0))
    bvec = pl.BlockSpec((1, 1, d), lambda i: (i // per_seq, 0, 0))
    return pl.pallas_call(
        _norm_mod_kernel,
        grid=(m // tm,),
        in_specs=[row, vec, bvec, bvec],
        out_specs=row,
        out_shape=jax.ShapeDtypeStruct((m, d), BF16),
        compiler_params=_params("parallel"),
        name="norm_mod",
    )(x, w.reshape(1, d), scale, shift)


def _resid_norm_kernel(y_ref, x_ref, pw_ref, g_ref, nw_ref, sc_ref, sh_ref, xo_ref, h_ref):
    xn = x_ref[...] + g_ref[0] * (_rms(y_ref[...].astype(F32)) * pw_ref[...])
    xo_ref[...] = xn
    h = _rms(xn) * nw_ref[...]
    h_ref[...] = (h * (1.0 + sc_ref[0]) + sh_ref[0]).astype(h_ref.dtype)


def _resid_kernel(y_ref, x_ref, pw_ref, g_ref, xo_ref):
    xo_ref[...] = x_ref[...] + g_ref[0] * (_rms(y_ref[...].astype(F32)) * pw_ref[...])


def _resid_norm(y, x, post_w, gate, seq, nxt=None):
    m, d = x.shape
    tm = min(ELEMENTWISE_ROWS, seq)
    per_seq = seq // tm
    row = pl.BlockSpec((tm, d), lambda i: (i, 0))
    vec = pl.BlockSpec((1, d), lambda i: (0, 0))
    bvec = pl.BlockSpec((1, 1, d), lambda i: (i // per_seq, 0, 0))
    if nxt is None:
        return pl.pallas_call(
            _resid_kernel,
            grid=(m // tm,),
            in_specs=[row, row, vec, bvec],
            out_specs=row,
            out_shape=jax.ShapeDtypeStruct((m, d), F32),
            compiler_params=_params("parallel"),
            name="resid",
        )(y, x, post_w.reshape(1, d), gate)
    norm_w, scale, shift = nxt
    return pl.pallas_call(
        _resid_norm_kernel,
        grid=(m // tm,),
        in_specs=[row, row, vec, bvec, vec, bvec, bvec],
        out_specs=[row, row],
        out_shape=[jax.ShapeDtypeStruct((m, d), F32), jax.ShapeDtypeStruct((m, d), BF16)],
        compiler_params=_params("parallel"),
        name="resid_norm",
    )(y, x, post_w.reshape(1, d), gate, norm_w.reshape(1, d), scale, shift)


def _mm_kernel(a_ref, w_ref, o_ref):
    o_ref[...] = _dot(a_ref[...], w_ref[...].astype(BF16)).astype(o_ref.dtype)


def _matmul(a, w, layer, n, out_dtype, tm, tn, name, rows_outer=False):
    m, k = a.shape
    tm, tn = min(tm, m), min(tn, n)
    if rows_outer:
        grid, ij = (m // tm, n // tn), lambda i, j: (i, j)
    else:
        grid, ij = (n // tn, m // tm), lambda j, i: (i, j)
    return pl.pallas_call(
        _mm_kernel,
        grid=grid,
        in_specs=[pl.BlockSpec((tm, k), lambda *g: (ij(*g)[0], 0)),
                  pl.BlockSpec((None, k, tn), lambda *g: (layer, 0, ij(*g)[1]))],
        out_specs=pl.BlockSpec((tm, tn), lambda *g: ij(*g)),
        out_shape=jax.ShapeDtypeStruct((m, n), out_dtype),
        compiler_params=_params("parallel", "arbitrary"),
        name=name,
    )(a, w)


def _mm_nt_kernel(a_ref, wt_ref, o_ref):
    o_ref[...] = _dot_nt(a_ref[...], wt_ref[0].astype(BF16)).astype(o_ref.dtype)


def _matmul_nt(a, wt, layer, n, skip_at, skip, out_dtype, tm, tn, name):
    m, k = a.shape
    tm, tn = min(tm, m), min(tn, n)
    sublanes = 8 * 4 // wt.dtype.itemsize
    assert skip % sublanes == 0 and n % tn == 0 and skip_at % tn == 0
    first_after = skip_at // tn
    row = lambda j: pl.multiple_of(j * tn + jnp.where(j >= first_after, skip, 0), sublanes)
    return pl.pallas_call(
        _mm_nt_kernel,
        grid=(m // tm, n // tn),
        in_specs=[pl.BlockSpec((tm, k), lambda i, j: (i, 0), pipeline_mode=pl.Buffered(1)),
                  pl.BlockSpec((pl.Element(1), pl.Element(tn), pl.Element(k)),
                               lambda i, j: (layer, row(j), 0))],
        out_specs=pl.BlockSpec((tm, tn), lambda i, j: (i, j)),
        out_shape=jax.ShapeDtypeStruct((m, n), out_dtype),
        compiler_params=_params("parallel", "arbitrary"),
        name=name,
    )(a, wt)


def _ba_kernel(heads, h_ref, w_ref, alog_ref, dtb_ref, o_ref):
    p = _dot_nt(h_ref[...], w_ref[...].astype(BF16))
    lane = lax.broadcasted_iota(jnp.int32, p.shape, 1)
    is_g = (lane >= heads) & (lane < 2 * heads)
    g = jnp.where(is_g, -jnp.exp(alog_ref[...]) * _softplus(p + dtb_ref[...]), 0.0)
    pos = lax.broadcasted_iota(jnp.int32, p.shape, 0) % CHUNK
    shift = 1
    while shift < CHUNK:
        g = g + jnp.where(pos >= shift, pltpu.roll(g, shift, axis=0), 0.0)
        shift *= 2
    o_ref[...] = jnp.where(is_g, g, _sigmoid(p))


def _ba(h, wt, layer, row0, a_log, dt_bias, heads):
    m, k = h.shape
    tm = min(GATE_ROWS, m)
    assert row0 % LANES == 0 and 2 * heads <= LANES
    pad = lambda v: jnp.zeros((1, LANES), F32).at[0, heads:2 * heads].set(v)
    return pl.pallas_call(
        functools.partial(_ba_kernel, heads),
        grid=(m // tm,),
        in_specs=[pl.BlockSpec((tm, k), lambda i: (i, 0)),
                  pl.BlockSpec((None, LANES, k), lambda i: (layer, row0 // LANES, 0)),
                  pl.BlockSpec((1, LANES), lambda i: (0, 0)),
                  pl.BlockSpec((1, LANES), lambda i: (0, 0))],
        out_specs=pl.BlockSpec((tm, LANES), lambda i: (i, 0)),
        out_shape=jax.ShapeDtypeStruct((m, LANES), F32),
        compiler_params=_params("parallel"),
        name="dn_gates",
    )(h, wt, pad(a_log), pad(dt_bias))


def _band(tm, lo, hi):
    ri = lax.broadcasted_iota(jnp.int32, (tm, tm + HALO), 0)
    ci = lax.broadcasted_iota(jnp.int32, (tm, tm + HALO), 1)
    back = ri + HALO - ci
    return jnp.where((back >= lo) & (back <= hi), 1.0, 0.0).astype(BF16)


def _pool_kernel(per_seq, u_ref, halo_ref, pw_ref, sc_ref, o_ref, ext_ref):
    tm, width = u_ref.shape
    group = width // len(POOL_WINDOWS)
    t = pl.program_id(0) % per_seq
    ext_ref[0:HALO, :] = jnp.where(t == 0, jnp.zeros(halo_ref.shape, halo_ref.dtype), halo_ref[...])
    ext_ref[HALO:, :] = u_ref[...]
    pos = t * tm + lax.broadcasted_iota(jnp.int32, (tm, 1), 0)
    for gi, win in enumerate(POOL_WINDOWS):
        cols = slice(gi * group, (gi + 1) * group)
        window_sum = _dot(_band(tm, 0, win - 1), ext_ref[:, cols])
        inv_count = 1.0 / jnp.minimum(pos + 1, win).astype(F32)
        pa = window_sum * inv_count - u_ref[:, cols].astype(F32)
        r = _dot(pa.astype(BF16), pw_ref[gi].astype(BF16)) * sc_ref[:, cols]
        o_ref[:, cols] = r.astype(o_ref.dtype)


def _pool(proj, pool_w, layer, pool_scale, seq):
    m = proj.shape[0]
    _, groups, group, _ = pool_w.shape
    width = groups * group
    tm = min(ELEMENTWISE_ROWS, seq)
    per_seq = seq // tm
    return pl.pallas_call(
        functools.partial(_pool_kernel, per_seq),
        grid=(m // tm,),
        in_specs=[pl.BlockSpec((tm, width), lambda i: (i, 0)),
                  pl.BlockSpec((HALO, width), lambda i: (jnp.maximum(i * (tm // HALO) - 1, 0), 0)),
                  pl.BlockSpec((None, groups, group, group), lambda i: (layer, 0, 0, 0)),
                  pl.BlockSpec((1, width), lambda i: (0, 0))],
        out_specs=pl.BlockSpec((tm, width), lambda i: (i, 0)),
        out_shape=jax.ShapeDtypeStruct((m, width), BF16),
        scratch_shapes=[pltpu.VMEM((tm + HALO, width), BF16)],
        compiler_params=_params("parallel"),
        name="pool_mixer",
    )(proj, proj, pool_w, pool_scale.reshape(1, width))


def _qkv_kernel(per_seq, q_ref, k_ref, v_ref, hq_ref, hk_ref, hv_ref, wq_ref, wk_ref, wv_ref,
                qo_ref, ko_ref, vo_ref, ext_ref):
    tm, width = q_ref.shape
    first = pl.program_id(0) % per_seq == 0
    taps = DN_CONV_TAPS
    shifts = jnp.concatenate([_band(tm, s, s) for s in range(1, taps)], axis=0)
    chunk = 2 * HEAD_DIM
    for x_ref, halo_ref, w_ref, o_ref, norm in (
            (q_ref, hq_ref, wq_ref, qo_ref, HEAD_DIM ** -0.5),
            (k_ref, hk_ref, wk_ref, ko_ref, 1.0),
            (v_ref, hv_ref, wv_ref, vo_ref, None)):
        ext_ref[0:HALO, :] = jnp.where(first, jnp.zeros(halo_ref.shape, halo_ref.dtype), halo_ref[...])
        ext_ref[HALO:, :] = x_ref[...]
        for c0 in range(0, width, chunk):
            wide = slice(c0, c0 + chunk)
            back = _dot(shifts, ext_ref[:, wide])
            y = x_ref[:, wide].astype(F32) * w_ref[taps - 1:taps, wide]
            for s in range(1, taps):
                y = y + back[(s - 1) * tm:s * tm] * w_ref[taps - 1 - s:taps - s, wide]
            y = _silu(y)
            for c in range(0, chunk, HEAD_DIM):
                yh = y[:, c:c + HEAD_DIM]
                if norm is not None:
                    yh = yh * (lax.rsqrt(jnp.sum(yh * yh, axis=-1, keepdims=True) + EPS) * norm)
                o_ref[:, c0 + c:c0 + c + HEAD_DIM] = yh.astype(o_ref.dtype)


def _qkv_conv(proj, dn_conv, width, col0, seq):
    m = proj.shape[0]
    tm = min(CONV_ROWS, seq)
    per_seq = seq // tm
    b0 = col0 // width
    rows = [pl.BlockSpec((tm, width), functools.partial(lambda s, i: (i, b0 + s), s)) for s in range(3)]
    halos = [pl.BlockSpec((HALO, width),
                          functools.partial(lambda s, i: (jnp.maximum(i * (tm // HALO) - 1, 0), b0 + s), s))
             for s in range(3)]
    taps = [pl.BlockSpec((DN_CONV_TAPS, width), functools.partial(lambda s, i: (0, s), s)) for s in range(3)]
    out = pl.BlockSpec((tm, width), lambda i: (i, 0))
    shp = jax.ShapeDtypeStruct((m, width), BF16)
    return pl.pallas_call(
        functools.partial(_qkv_kernel, per_seq),
        grid=(m // tm,),
        in_specs=rows + halos + taps,
        out_specs=[out, out, out],
        out_shape=[shp, shp, shp],
        scratch_shapes=[pltpu.VMEM((tm + HALO, width), BF16)],
        compiler_params=_params("parallel"),
        name="dn_qkv_conv",
    )(proj, proj, proj, proj, proj, proj, dn_conv, dn_conv, dn_conv)


def _unit_lower_inverses(lows):
    n = lows[0].shape[0]
    ri = lax.broadcasted_iota(jnp.int32, (n, n), 0)
    ci = lax.broadcasted_iota(jnp.int32, (n, n), 1)
    same = lambda bits: jnp.right_shift(ri, bits) == jnp.right_shift(ci, bits)
    eye = jnp.where(ri == ci, 1.0, 0.0)
    invs = [eye - jnp.where(same(1), low, 0.0) for low in lows]
    bits = 1
    while (1 << bits) < n:
        mask = same(bits + 1) & jnp.logical_not(same(bits))
        inv16 = [inv.astype(BF16) for inv in invs]
        right = [_dot(jnp.where(mask, low, 0.0).astype(BF16), i16) for low, i16 in zip(lows, inv16)]
        invs = [inv - _dot(i16, r.astype(BF16)) for inv, i16, r in zip(invs, inv16, right)]
        bits += 1
    return invs


def _ride_along_blocks(rows, steps):
    block = next(b for b in range(16, rows + 1, 16) if rows % b == 0 and rows // b <= steps)
    return block, rows // block


def _delta_kernel(heads, q_ref, k_ref, v_ref, z_ref, g_ref, gt_ref, nw_ref, wsrc_ref, o_ref, w16_ref,
                  state_ref):
    rows, width = q_ref.shape
    hp = width // HEAD_DIM
    c = CHUNK
    w16_ref[...] = wsrc_ref[...].astype(BF16)

    @pl.when(pl.program_id(2) == 0)
    def _():
        state_ref[...] = jnp.zeros_like(state_ref)

    gates = g_ref[...]
    lane = lax.broadcasted_iota(jnp.int32, gates.shape, 1)
    ri = lax.broadcasted_iota(jnp.int32, (c, c), 0)
    ci = lax.broadcasted_iota(jnp.int32, (c, c), 1)
    causal = ri >= ci
    strict = ri > ci

    work = []
    for hh in range(hp):
        head = pl.program_id(1) * hp + hh
        cols = slice(hh * HEAD_DIM, (hh + 1) * HEAD_DIM)
        beta_col = jnp.sum(jnp.where(lane == head, gates, 0.0), axis=-1, keepdims=True)
        gc_col = jnp.sum(jnp.where(lane == heads + head, gates, 0.0), axis=-1, keepdims=True)
        gc_row = gt_ref[pl.ds(heads + head, 1), :]
        for r0 in range(0, rows, c):
            rsl = slice(r0, r0 + c)
            q = q_ref[rsl, cols]
            k16 = k_ref[rsl, cols]
            k = k16.astype(F32)
            beta = beta_col[rsl]
            gcc = gc_col[rsl]
            gcr = gc_row[:, rsl]
            g_last = gcr[:, c - 1:c]
            decay = jnp.exp(jnp.where(causal, gcc - gcr, -jnp.inf))
            e_g = jnp.exp(gcc)
            kb = k * beta
            kq_k = _dot_nt(jnp.concatenate([kb.astype(BF16), q], axis=0), k16)
            low = jnp.where(strict, kq_k[:c] * decay, 0.0)
            qk = jnp.where(causal, kq_k[c:] * decay, 0.0)
            work.append(dict(
                hh=hh, rsl=rsl, cols=cols, low=low,
                vb_kbg16=jnp.concatenate([(v_ref[rsl, cols].astype(F32) * beta).astype(BF16),
                                          (kb * e_g).astype(BF16)], axis=1),
                qd16=(q.astype(F32) * e_g).astype(BF16),
                qk_kdt16=jnp.concatenate([qk.astype(BF16),
                                          (k * jnp.exp(g_last - gcc)).T.astype(BF16)], axis=0),
                g_end=jnp.exp(g_last)))
    for item, t_inv in zip(work, _unit_lower_inverses([item["low"] for item in work])):
        uw = _dot(t_inv.astype(BF16), item["vb_kbg16"])
        item["u"] = uw[:, :HEAD_DIM]
        item["w_qd16"] = jnp.concatenate([uw[:, HEAD_DIM:].astype(BF16), item["qd16"]], axis=0)

    for hh in range(hp):
        state = state_ref[hh]
        for item in work:
            if item["hh"] != hh:
                continue
            from_state = _dot(item["w_qd16"], state.astype(BF16))
            vn16 = (item["u"] - from_state[:c]).astype(BF16)
            from_new = _dot(item["qk_kdt16"], vn16)
            item["o"] = from_state[c:] + from_new[:c]
            state = state * item["g_end"] + from_new[c:]
        state_ref[hh] = state

    for item in work:
        rsl, cols = item["rsl"], item["cols"]
        z = z_ref[rsl, cols].astype(F32)
        o_ref[rsl, cols] = (_rms(item["o"]) * nw_ref[...] * _silu(z)).astype(o_ref.dtype)


def _delta(q, k, v, proj, z_col0, gates, gates_t, dn_norm, batch, seq, heads, w_src, layer):
    m, width = q.shape
    hp = min(DELTA_HEADS_PER_STEP, heads)
    rows = min(CHUNK, seq)
    per_seq = seq // rows
    groups = heads // hp
    bw = hp * HEAD_DIM
    zb0 = z_col0 // bw
    _, w_rows, w_cols = w_src.shape
    w_block, w_blocks = _ride_along_blocks(w_rows, batch * groups * per_seq)
    w_idx = lambda b, h, t: jnp.minimum((b * groups + h) * per_seq + t, w_blocks - 1)
    qkv = pl.BlockSpec((rows, bw), lambda b, h, t: (b * per_seq + t, h))
    return pl.pallas_call(
        functools.partial(_delta_kernel, heads),
        grid=(batch, groups, per_seq),
        in_specs=[qkv, qkv, qkv,
                  pl.BlockSpec((rows, bw), lambda b, h, t: (b * per_seq + t, zb0 + h)),
                  pl.BlockSpec((rows, LANES), lambda b, h, t: (b * per_seq + t, 0)),
                  pl.BlockSpec((2 * heads, rows), lambda b, h, t: (0, b * per_seq + t)),
                  pl.BlockSpec((1, HEAD_DIM), lambda b, h, t: (0, 0)),
                  pl.BlockSpec((None, w_block, w_cols), lambda b, h, t: (layer, w_idx(b, h, t), 0))],
        out_specs=[qkv, pl.BlockSpec((w_block, w_cols), lambda b, h, t: (w_idx(b, h, t), 0))],
        out_shape=[jax.ShapeDtypeStruct((m, width), BF16), jax.ShapeDtypeStruct((w_rows, w_cols), BF16)],
        scratch_shapes=[pltpu.VMEM((hp, HEAD_DIM, HEAD_DIM), F32)],
        compiler_params=_params("arbitrary", "arbitrary", "arbitrary"),
        name="gated_delta",
    )(q, k, v, proj, gates, gates_t, dn_norm.reshape(1, HEAD_DIM), w_src)


def _merge_kernel(ga_ref, gb_ref, pa_ref, ob_ref, pp_ref, dp_ref, y_ref):
    ya = _sigmoid(ga_ref[...].astype(F32)) * _dot(pa_ref[...], pp_ref[...].astype(BF16))
    yb = _sigmoid(gb_ref[...].astype(F32)) * _dot(ob_ref[...], dp_ref[...].astype(BF16))
    y_ref[...] = (ya + yb).astype(y_ref.dtype)


def _merge(proj, gate_col0, pa, ob, pool_proj, dn_proj, layer):
    m, kp = pa.shape
    d = pool_proj.shape[2]
    tm, tn = min(MATMUL_ROWS, m), min(MATMUL_COLS, d)
    nb = d // tn
    assert gate_col0 % tn == 0
    g0 = gate_col0 // tn
    return pl.pallas_call(
        _merge_kernel,
        grid=(nb, m // tm),
        in_specs=[pl.BlockSpec((tm, tn), lambda j, i: (i, g0 + j)),
                  pl.BlockSpec((tm, tn), lambda j, i: (i, g0 + nb + j)),
                  pl.BlockSpec((tm, kp), lambda j, i: (i, 0)),
                  pl.BlockSpec((tm, kp), lambda j, i: (i, 0)),
                  pl.BlockSpec((None, kp, tn), lambda j, i: (layer, 0, j)),
                  pl.BlockSpec((None, kp, tn), lambda j, i: (layer, 0, j))],
        out_specs=pl.BlockSpec((tm, tn), lambda j, i: (i, j)),
        out_shape=jax.ShapeDtypeStruct((m, d), BF16),
        compiler_params=_params("parallel", "arbitrary"),
        name="gated_merge",
    )(proj, proj, pa, ob, pool_proj, dn_proj)


FFN_CARRY = 8


def _ffn_up_kernel(per_seq, h_ref, wa_ref, wu0_ref, wu1_ref, cw_ref, cb_ref, wsrc_ref, o_ref, w16_ref, a_ref):
    tm = h_ref.shape[0]
    w16_ref[...] = wsrc_ref[...].astype(BF16)

    @pl.when(pl.program_id(1) % per_seq == 0)
    def _():
        a_ref[0:FFN_CARRY, :] = jnp.zeros((FFN_CARRY, a_ref.shape[1]), F32)

    h = h_ref[...]
    a_ref[FFN_CARRY:, :] = _dot(h, wa_ref[...])
    u = jnp.concatenate([_dot(h, wu0_ref[...]), _dot(h, wu1_ref[...])], axis=1)
    a = cb_ref[...]
    for tap in range(FFN_CONV_TAPS):
        r0 = FFN_CARRY - (FFN_CONV_TAPS - 1) + tap
        a = a + a_ref[r0:r0 + tm, :] * cw_ref[tap:tap + 1, :]
    o_ref[...] = (_silu(a) * u).astype(o_ref.dtype)
    a_ref[0:FFN_CARRY, :] = a_ref[tm:tm + FFN_CARRY, :]


def _ffn_up(h, w_up, layer, conv_w, conv_b, seq, w_src):
    m, d = h.shape
    n = w_up.shape[1] // 2
    tm, tn = min(MATMUL_ROWS, seq), MATMUL_COLS
    half = tn // 2
    assert n % half == 0
    nh = n // half
    nt = pl.cdiv(n, tn)
    per_seq = seq // tm
    pad = nt * tn - n
    conv_w = jnp.pad(conv_w, ((0, 0), (0, 0), (0, pad)))
    conv_b = jnp.pad(conv_b, ((0, 0), (0, pad))).reshape(conv_b.shape[0], 1, nt * tn)
    row_tiles = m // tm
    _, w_rows, w_cols = w_src.shape
    w_block, w_blocks = _ride_along_blocks(w_rows, nt * row_tiles)
    w_idx = lambda j, i: jnp.minimum(j * row_tiles + i, w_blocks - 1)
    return pl.pallas_call(
        functools.partial(_ffn_up_kernel, per_seq),
        grid=(nt, row_tiles),
        in_specs=[pl.BlockSpec((tm, d), lambda j, i: (i, 0)),
                  pl.BlockSpec((d, tn), lambda j, i: (0, j)),
                  pl.BlockSpec((d, half), lambda j, i: (0, nh + 2 * j)),
                  pl.BlockSpec((d, half), lambda j, i: (0, jnp.minimum(nh + 2 * j + 1, 2 * nh - 1))),
                  pl.BlockSpec((None, FFN_CONV_TAPS, tn), lambda j, i: (layer, 0, j)),
                  pl.BlockSpec((None, 1, tn), lambda j, i: (layer, 0, j)),
                  pl.BlockSpec((None, w_block, w_cols), lambda j, i: (layer, w_idx(j, i), 0))],
        out_specs=[pl.BlockSpec((tm, tn), lambda j, i: (i, j)),
                   pl.BlockSpec((w_block, w_cols), lambda j, i: (w_idx(j, i), 0))],
        out_shape=[jax.ShapeDtypeStruct((m, n), BF16), jax.ShapeDtypeStruct((w_rows, w_cols), BF16)],
        scratch_shapes=[pltpu.VMEM((tm + FFN_CARRY, tn), F32)],
        compiler_params=_params("arbitrary", "arbitrary"),
        name="ffn_up_conv_gate",
    )(h, w_up, w_up, w_up, conv_w, conv_b, w_src)


def kernel(x, c, w_ada, b_ada, ada_table, mix_pre_norm, w_in, pool_w, pool_scale, pool_proj,
           dn_conv, dn_a_log, dn_dt_bias, dn_norm, dn_proj, w_o, mix_post_norm,
           ffn_pre_norm, w_up, ffn_conv, ffn_conv_b, w_down, ffn_post_norm):
    batch, seq, d = x.shape
    depth = w_in.shape[0]
    m = batch * seq
    pool_width = pool_scale.shape[1]
    dn_width = dn_proj.shape[1]
    heads = dn_a_log.shape[1]
    c_qkv = pool_width
    c_z = c_qkv + 3 * dn_width
    c_ba = c_z + dn_width

    c8 = jnp.pad(c, ((0, SUBLANES - batch), (0, 0)))
    mod = _ada(c8, w_ada, b_ada, ada_table.reshape(depth, N_MOD * d))
    mod = mod[:, :batch].reshape(depth, batch, N_MOD, 1, d)

    xf = x.reshape(m, d)
    h = _norm_mod(xf, mix_pre_norm[0], mod[0, :, 1], mod[0, :, 0], seq)
    w_in_t = jnp.swapaxes(w_in, 1, 2)
    for l in range(depth):
        shift_f, scale_f, gate_m, gate_f = mod[l, :, 3], mod[l, :, 4], mod[l, :, 2], mod[l, :, 5]
        proj = _matmul_nt(h, w_in_t, l, c_ba + 2 * d, c_ba, 2 * heads, BF16, IN_PROJ_ROWS, MATMUL_COLS, "in_proj")
        gates = _ba(h, w_in_t, l, c_ba, dn_a_log[l], dn_dt_bias[l], heads)
        gates_t = gates[:, :2 * heads].T
        pa = _pool(proj, pool_w, l, pool_scale[l], seq)
        q, k, v = _qkv_conv(proj, dn_conv[l], dn_width, c_qkv, seq)
        ob, w_up16 = _delta(q, k, v, proj, c_z, gates, gates_t, dn_norm[l], batch, seq, heads, w_up, l)
        y = _merge(proj, c_ba, pa, ob, pool_proj, dn_proj, l)
        yo = _matmul(y, w_o, l, d, BF16, MATMUL_ROWS, MATMUL_COLS, "out_proj", rows_outer=True)
        xf, h = _resid_norm(yo, xf, mix_post_norm[l], gate_m, seq,
                            nxt=(ffn_pre_norm[l], scale_f, shift_f))
        act, w_down16 = _ffn_up(h, w_up16, l, ffn_conv, ffn_conv_b, seq, w_down)
        yd = _matmul(act, w_down16[None], 0, d, BF16, FFN_DOWN_ROWS, MATMUL_COLS, "ffn_down")
        if l + 1 < depth:
            xf, h = _resid_norm(yd, xf, ffn_post_norm[l], gate_f, seq,
                                nxt=(mix_pre_norm[l + 1], mod[l + 1, :, 1], mod[l + 1, :, 0]))
        else:
            xf = _resid_norm(yd, xf, ffn_post_norm[l], gate_f, seq)
    return xf.reshape(batch, seq, d)
```

```python
import functools

import jax
import jax.numpy as jnp
from jax import lax
from jax.experimental import pallas as pl
from jax.experimental.pallas import tpu as pltpu

F32 = jnp.float32
BF16 = jnp.bfloat16

EPS = 1e-6
POOL_WINDOWS = (2, 4, 8, 16)
HEAD_DIM = 128
DN_CONV_TAPS = 4
FFN_CONV_TAPS = 3
N_MOD = 6
CHUNK = 128
HALO = 16
VMEM_LIMIT_BYTES = 56 * 1024 * 1024
LANES = 128
SUBLANES = 8

ELEMENTWISE_ROWS = 256
CONV_ROWS = 128
GATE_ROWS = 512
MATMUL_ROWS, MATMUL_COLS = 1024, 512
IN_PROJ_ROWS = 2048
FFN_DOWN_ROWS = 512
DELTA_HEADS_PER_STEP = 16
ADA_COLS = 512


def _params(*semantics):
    return pltpu.CompilerParams(dimension_semantics=semantics,
                                vmem_limit_bytes=VMEM_LIMIT_BYTES)


def _dot(a, b):
    return jnp.dot(a, b, preferred_element_type=F32)


def _dot_nt(a, b):
    return lax.dot_general(a, b, (((1,), (1,)), ((), ())), preferred_element_type=F32)


def _sigmoid(x):
    return 1.0 / (1.0 + jnp.exp(-x))


def _silu(x):
    return x * _sigmoid(x)


def _softplus(x):
    return jnp.maximum(x, 0.0) + jnp.log1p(jnp.exp(-jnp.abs(x)))


def _rms(x):
    return x * lax.rsqrt(jnp.mean(x * x, axis=-1, keepdims=True) + EPS)


def _ada_kernel(c_ref, w_ref, b_ref, t_ref, o_ref):
    a = _silu(c_ref[...]).astype(BF16)
    acc = _dot(a, w_ref[...].astype(BF16)) + b_ref[...]
    o_ref[...] = acc[None] + t_ref[...]


def _ada(c8, w_ada, b_ada, table):
    depth = table.shape[0]
    d, n = w_ada.shape
    tn = min(ADA_COLS, n)
    return pl.pallas_call(
        _ada_kernel,
        grid=(n // tn,),
        in_specs=[pl.BlockSpec((SUBLANES, d), lambda j: (0, 0)),
                  pl.BlockSpec((d, tn), lambda j: (0, j)),
                  pl.BlockSpec((1, tn), lambda j: (0, j)),
                  pl.BlockSpec((depth, 1, tn), lambda j: (0, 0, j))],
        out_specs=pl.BlockSpec((depth, SUBLANES, tn), lambda j: (0, 0, j)),
        out_shape=jax.ShapeDtypeStruct((depth, SUBLANES, n), F32),
        compiler_params=_params("arbitrary"),
        name="ada",
    )(c8, w_ada, b_ada.reshape(1, n), table.reshape(depth, 1, n))


def _norm_mod_kernel(x_ref, w_ref, sc_ref, sh_ref, h_ref):
    y = _rms(x_ref[...]) * w_ref[...]
    h_ref[...] = (y * (1.0 + sc_ref[0]) + sh_ref[0]).astype(h_ref.dtype)


def _norm_mod(x, w, scale, shift, seq):
    m, d = x.shape
    tm = min(ELEMENTWISE_ROWS, seq)
    per_seq = seq // tm
    row = pl.BlockSpec((tm, d), lambda i: (i, 0))
    vec = pl.BlockSpec((1, d), lambda i: (0, 0))
    bvec = pl.BlockSpec((1, 1, d), lambda i: (i // per_seq, 0, 0))
    return pl.pallas_call(
        _norm_mod_kernel,
        grid=(m // tm,),
        in_specs=[row, vec, bvec, bvec],
        out_specs=row,
        out_shape=jax.ShapeDtypeStruct((m, d), BF16),
        compiler_params=_params("parallel"),
        name="norm_mod",
    )(x, w.reshape(1, d), scale, shift)


def _resid_norm_kernel(y_ref, x_ref, pw_ref, g_ref, nw_ref, sc_ref, sh_ref, xo_ref, h_ref):
    xn = x_ref[...] + g_ref[0] * (_rms(y_ref[...].astype(F32)) * pw_ref[...])
    xo_ref[...] = xn
    h = _rms(xn) * nw_ref[...]
    h_ref[...] = (h * (1.0 + sc_ref[0]) + sh_ref[0]).astype(h_ref.dtype)


def _resid_kernel(y_ref, x_ref, pw_ref, g_ref, xo_ref):
    xo_ref[...] = x_ref[...] + g_ref[0] * (_rms(y_ref[...].astype(F32)) * pw_ref[...])


def _resid_norm(y, x, post_w, gate, seq, nxt=None):
    m, d = x.shape
    tm = min(ELEMENTWISE_ROWS, seq)
    per_seq = seq // tm
    row = pl.BlockSpec((tm, d), lambda i: (i, 0))
    vec = pl.BlockSpec((1, d), lambda i: (0, 0))
    bvec = pl.BlockSpec((1, 1, d), lambda i: (i // per_seq, 0, 0))
    if nxt is None:
        return pl.pallas_call(
            _resid_kernel,
            grid=(m // tm,),
            in_specs=[row, row, vec, bvec],
            out_specs=row,
            out_shape=jax.ShapeDtypeStruct((m, d), F32),
            compiler_params=_params("parallel"),
            name="resid",
        )(y, x, post_w.reshape(1, d), gate)
    norm_w, scale, shift = nxt
    return pl.pallas_call(
        _resid_norm_kernel,
        grid=(m // tm,),
        in_specs=[row, row, vec, bvec, vec, bvec, bvec],
        out_specs=[row, row],
        out_shape=[jax.ShapeDtypeStruct((m, d), F32), jax.ShapeDtypeStruct((m, d), BF16)],
        compiler_params=_params("parallel"),
        name="resid_norm",
    )(y, x, post_w.reshape(1, d), gate, norm_w.reshape(1, d), scale, shift)


def _mm_kernel(a_ref, w_ref, o_ref):
    o_ref[...] = _dot(a_ref[...], w_ref[...].astype(BF16)).astype(o_ref.dtype)


def _matmul(a, w, layer, n, out_dtype, tm, tn, name, rows_outer=False):
    m, k = a.shape
    tm, tn = min(tm, m), min(tn, n)
    if rows_outer:
        grid, ij = (m // tm, n // tn), lambda i, j: (i, j)
    else:
        grid, ij = (n // tn, m // tm), lambda j, i: (i, j)
    return pl.pallas_call(
        _mm_kernel,
        grid=grid,
        in_specs=[pl.BlockSpec((tm, k), lambda *g: (ij(*g)[0], 0)),
                  pl.BlockSpec((None, k, tn), lambda *g: (layer, 0, ij(*g)[1]))],
        out_specs=pl.BlockSpec((tm, tn), lambda *g: ij(*g)),
        out_shape=jax.ShapeDtypeStruct((m, n), out_dtype),
        compiler_params=_params("parallel", "arbitrary"),
        name=name,
    )(a, w)


def _mm_nt_kernel(a_ref, wt_ref, o_ref):
    o_ref[...] = _dot_nt(a_ref[...], wt_ref[0].astype(BF16)).astype(o_ref.dtype)


def _matmul_nt(a, wt, layer, n, skip_at, skip, out_dtype, tm, tn, name):
    m, k = a.shape
    tm, tn = min(tm, m), min(tn, n)
    sublanes = 8 * 4 // wt.dtype.itemsize
    assert skip % sublanes == 0 and n % tn == 0 and skip_at % tn == 0
    first_after = skip_at // tn
    row = lambda j: pl.multiple_of(j * tn + jnp.where(j >= first_after, skip, 0), sublanes)
    return pl.pallas_call(
        _mm_nt_kernel,
        grid=(m // tm, n // tn),
        in_specs=[pl.BlockSpec((tm, k), lambda i, j: (i, 0), pipeline_mode=pl.Buffered(1)),
                  pl.BlockSpec((pl.Element(1), pl.Element(tn), pl.Element(k)),
                               lambda i, j: (layer, row(j), 0))],
        out_specs=pl.BlockSpec((tm, tn), lambda i, j: (i, j)),
        out_shape=jax.ShapeDtypeStruct((m, n), out_dtype),
        compiler_params=_params("parallel", "arbitrary"),
        name=name,
    )(a, wt)


def _ba_kernel(heads, h_ref, w_ref, alog_ref, dtb_ref, o_ref):
    p = _dot_nt(h_ref[...], w_ref[...].astype(BF16))
    lane = lax.broadcasted_iota(jnp.int32, p.shape, 1)
    is_g = (lane >= heads) & (lane < 2 * heads)
    g = jnp.where(is_g, -jnp.exp(alog_ref[...]) * _softplus(p + dtb_ref[...]), 0.0)
    pos = lax.broadcasted_iota(jnp.int32, p.shape, 0) % CHUNK
    shift = 1
    while shift < CHUNK:
        g = g + jnp.where(pos >= shift, pltpu.roll(g, shift, axis=0), 0.0)
        shift *= 2
    o_ref[...] = jnp.where(is_g, g, _sigmoid(p))


def _ba(h, wt, layer, row0, a_log, dt_bias, heads):
    m, k = h.shape
    tm = min(GATE_ROWS, m)
    assert row0 % LANES == 0 and 2 * heads <= LANES
    pad = lambda v: jnp.zeros((1, LANES), F32).at[0, heads:2 * heads].set(v)
    return pl.pallas_call(
        functools.partial(_ba_kernel, heads),
        grid=(m // tm,),
        in_specs=[pl.BlockSpec((tm, k), lambda i: (i, 0)),
                  pl.BlockSpec((None, LANES, k), lambda i: (layer, row0 // LANES, 0)),
                  pl.BlockSpec((1, LANES), lambda i: (0, 0)),
                  pl.BlockSpec((1, LANES), lambda i: (0, 0))],
        out_specs=pl.BlockSpec((tm, LANES), lambda i: (i, 0)),
        out_shape=jax.ShapeDtypeStruct((m, LANES), F32),
        compiler_params=_params("parallel"),
        name="dn_gates",
    )(h, wt, pad(a_log), pad(dt_bias))


def _band(tm, lo, hi):
    ri = lax.broadcasted_iota(jnp.int32, (tm, tm + HALO), 0)
    ci = lax.broadcasted_iota(jnp.int32, (tm, tm + HALO), 1)
    back = ri + HALO - ci
    return jnp.where((back >= lo) & (back <= hi), 1.0, 0.0).astype(BF16)


def _pool_kernel(per_seq, u_ref, halo_ref, pw_ref, sc_ref, o_ref, ext_ref):
    tm, width = u_ref.shape
    group = width // len(POOL_WINDOWS)
    t = pl.program_id(0) % per_seq
    ext_ref[0:HALO, :] = jnp.where(t == 0, jnp.zeros(halo_ref.shape, halo_ref.dtype), halo_ref[...])
    ext_ref[HALO:, :] = u_ref[...]
    pos = t * tm + lax.broadcasted_iota(jnp.int32, (tm, 1), 0)
    for gi, win in enumerate(POOL_WINDOWS):
        cols = slice(gi * group, (gi + 1) * group)
        window_sum = _dot(_band(tm, 0, win - 1), ext_ref[:, cols])
        inv_count = 1.0 / jnp.minimum(pos + 1, win).astype(F32)
        pa = window_sum * inv_count - u_ref[:, cols].astype(F32)
        r = _dot(pa.astype(BF16), pw_ref[gi].astype(BF16)) * sc_ref[:, cols]
        o_ref[:, cols] = r.astype(o_ref.dtype)


def _pool(proj, pool_w, layer, pool_scale, seq):
    m = proj.shape[0]
    _, groups, group, _ = pool_w.shape
    width = groups * group
    tm = min(ELEMENTWISE_ROWS, seq)
    per_seq = seq // tm
    return pl.pallas_call(
        functools.partial(_pool_kernel, per_seq),
        grid=(m // tm,),
        in_specs=[pl.BlockSpec((tm, width), lambda i: (i, 0)),
                  pl.BlockSpec((HALO, width), lambda i: (jnp.maximum(i * (tm // HALO) - 1, 0), 0)),
                  pl.BlockSpec((None, groups, group, group), lambda i: (layer, 0, 0, 0)),
                  pl.BlockSpec((1, width), lambda i: (0, 0))],
        out_specs=pl.BlockSpec((tm, width), lambda i: (i, 0)),
        out_shape=jax.ShapeDtypeStruct((m, width), BF16),
        scratch_shapes=[pltpu.VMEM((tm + HALO, width), BF16)],
        compiler_params=_params("parallel"),
        name="pool_mixer",
    )(proj, proj, pool_w, pool_scale.reshape(1, width))


def _qkv_kernel(per_seq, q_ref, k_ref, v_ref, hq_ref, hk_ref, hv_ref, wq_ref, wk_ref, wv_ref,
                qo_ref, ko_ref, vo_ref, ext_ref):
    tm, width = q_ref.shape
    first = pl.program_id(0) % per_seq == 0
    taps = DN_CONV_TAPS
    shifts = jnp.concatenate([_band(tm, s, s) for s in range(1, taps)], axis=0)
    chunk = 2 * HEAD_DIM
    for x_ref, halo_ref, w_ref, o_ref, norm in (
            (q_ref, hq_ref, wq_ref, qo_ref, HEAD_DIM ** -0.5),
            (k_ref, hk_ref, wk_ref, ko_ref, 1.0),
            (v_ref, hv_ref, wv_ref, vo_ref, None)):
        ext_ref[0:HALO, :] = jnp.where(first, jnp.zeros(halo_ref.shape, halo_ref.dtype), halo_ref[...])
        ext_ref[HALO:, :] = x_ref[...]
        for c0 in range(0, width, chunk):
            wide = slice(c0, c0 + chunk)
            back = _dot(shifts, ext_ref[:, wide])
            y = x_ref[:, wide].astype(F32) * w_ref[taps - 1:taps, wide]
            for s in range(1, taps):
                y = y + back[(s - 1) * tm:s * tm] * w_ref[taps - 1 - s:taps - s, wide]
            y = _silu(y)
            for c in range(0, chunk, HEAD_DIM):
                yh = y[:, c:c + HEAD_DIM]
                if norm is not None:
                    yh = yh * (lax.rsqrt(jnp.sum(yh * yh, axis=-1, keepdims=True) + EPS) * norm)
                o_ref[:, c0 + c:c0 + c + HEAD_DIM] = yh.astype(o_ref.dtype)


def _qkv_conv(proj, dn_conv, width, col0, seq):
    m = proj.shape[0]
    tm = min(CONV_ROWS, seq)
    per_seq = seq // tm
    b0 = col0 // width
    rows = [pl.BlockSpec((tm, width), functools.partial(lambda s, i: (i, b0 + s), s)) for s in range(3)]
    halos = [pl.BlockSpec((HALO, width),
                          functools.partial(lambda s, i: (jnp.maximum(i * (tm // HALO) - 1, 0), b0 + s), s))
             for s in range(3)]
    taps = [pl.BlockSpec((DN_CONV_TAPS, width), functools.partial(lambda s, i: (0, s), s)) for s in range(3)]
    out = pl.BlockSpec((tm, width), lambda i: (i, 0))
    shp = jax.ShapeDtypeStruct((m, width), BF16)
    return pl.pallas_call(
        functools.partial(_qkv_kernel, per_seq),
        grid=(m // tm,),
        in_specs=rows + halos + taps,
        out_specs=[out, out, out],
        out_shape=[shp, shp, shp],
        scratch_shapes=[pltpu.VMEM((tm + HALO, width), BF16)],
        compiler_params=_params("parallel"),
        name="dn_qkv_conv",
    )(proj, proj, proj, proj, proj, proj, dn_conv, dn_conv, dn_conv)


def _unit_lower_inverses(lows):
    n = lows[0].shape[0]
    ri = lax.broadcasted_iota(jnp.int32, (n, n), 0)
    ci = lax.broadcasted_iota(jnp.int32, (n, n), 1)
    same = lambda bits: jnp.right_shift(ri, bits) == jnp.right_shift(ci, bits)
    eye = jnp.where(ri == ci, 1.0, 0.0)
    invs = [eye - jnp.where(same(1), low, 0.0) for low in lows]
    bits = 1
    while (1 << bits) < n:
        mask = same(bits + 1) & jnp.logical_not(same(bits))
        inv16 = [inv.astype(BF16) for inv in invs]
        right = [_dot(jnp.where(mask, low, 0.0).astype(BF16), i16) for low, i16 in zip(lows, inv16)]
        invs = [inv - _dot(i16, r.astype(BF16)) for inv, i16, r in zip(invs, inv16, right)]
        bits += 1
    return invs


def _ride_along_blocks(rows, steps):
    block = next(b for b in range(16, rows + 1, 16) if rows % b == 0 and rows // b <= steps)
    return block, rows // block


def _delta_kernel(heads, q_ref, k_ref, v_ref, z_ref, g_ref, gt_ref, nw_ref, wsrc_ref, o_ref, w16_ref,
                  state_ref):
    rows, width = q_ref.shape
    hp = width // HEAD_DIM
    c = CHUNK
    w16_ref[...] = wsrc_ref[...].astype(BF16)

    @pl.when(pl.program_id(2) == 0)
    def _():
        state_ref[...] = jnp.zeros_like(state_ref)

    gates = g_ref[...]
    lane = lax.broadcasted_iota(jnp.int32, gates.shape, 1)
    ri = lax.broadcasted_iota(jnp.int32, (c, c), 0)
    ci = lax.broadcasted_iota(jnp.int32, (c, c), 1)
    causal = ri >= ci
    strict = ri > ci

    work = []
    for hh in range(hp):
        head = pl.program_id(1) * hp + hh
        cols = slice(hh * HEAD_DIM, (hh + 1) * HEAD_DIM)
        beta_col = jnp.sum(jnp.where(lane == head, gates, 0.0), axis=-1, keepdims=True)
        gc_col = jnp.sum(jnp.where(lane == heads + head, gates, 0.0), axis=-1, keepdims=True)
        gc_row = gt_ref[pl.ds(heads + head, 1), :]
        for r0 in range(0, rows, c):
            rsl = slice(r0, r0 + c)
            q = q_ref[rsl, cols]
            k16 = k_ref[rsl, cols]
            k = k16.astype(F32)
            beta = beta_col[rsl]
            gcc = gc_col[rsl]
            gcr = gc_row[:, rsl]
            g_last = gcr[:, c - 1:c]
            decay = jnp.exp(jnp.where(causal, gcc - gcr, -jnp.inf))
            e_g = jnp.exp(gcc)
            kb = k * beta
            kq_k = _dot_nt(jnp.concatenate([kb.astype(BF16), q], axis=0), k16)
            low = jnp.where(strict, kq_k[:c] * decay, 0.0)
            qk = jnp.where(causal, kq_k[c:] * decay, 0.0)
            work.append(dict(
                hh=hh, rsl=rsl, cols=cols, low=low,
                vb_kbg16=jnp.concatenate([(v_ref[rsl, cols].astype(F32) * beta).astype(BF16),
                                          (kb * e_g).astype(BF16)], axis=1),
                qd16=(q.astype(F32) * e_g).astype(BF16),
                qk_kdt16=jnp.concatenate([qk.astype(BF16),
                                          (k * jnp.exp(g_last - gcc)).T.astype(BF16)], axis=0),
                g_end=jnp.exp(g_last)))
    for item, t_inv in zip(work, _unit_lower_inverses([item["low"] for item in work])):
        uw = _dot(t_inv.astype(BF16), item["vb_kbg16"])
        item["u"] = uw[:, :HEAD_DIM]
        item["w_qd16"] = jnp.concatenate([uw[:, HEAD_DIM:].astype(BF16), item["qd16"]], axis=0)

    for hh in range(hp):
        state = state_ref[hh]
        for item in work:
            if item["hh"] != hh:
                continue
            from_state = _dot(item["w_qd16"], state.astype(BF16))
            vn16 = (item["u"] - from_state[:c]).astype(BF16)
            from_new = _dot(item["qk_kdt16"], vn16)
            item["o"] = from_state[c:] + from_new[:c]
            state = state * item["g_end"] + from_new[c:]
        state_ref[hh] = state

    for item in work:
        rsl, cols = item["rsl"], item["cols"]
        z = z_ref[rsl, cols].astype(F32)
        o_ref[rsl, cols] = (_rms(item["o"]) * nw_ref[...] * _silu(z)).astype(o_ref.dtype)


def _delta(q, k, v, proj, z_col0, gates, gates_t, dn_norm, batch, seq, heads, w_src, layer):
    m, width = q.shape
    hp = min(DELTA_HEADS_PER_STEP, heads)
    rows = min(CHUNK, seq)
    per_seq = seq // rows
    groups = heads // hp
    bw = hp * HEAD_DIM
    zb0 = z_col0 // bw
    _, w_rows, w_cols = w_src.shape
    w_block, w_blocks = _ride_along_blocks(w_rows, batch * groups * per_seq)
    w_idx = lambda b, h, t: jnp.minimum((b * groups + h) * per_seq + t, w_blocks - 1)
    qkv = pl.BlockSpec((rows, bw), lambda b, h, t: (b * per_seq + t, h))
    return pl.pallas_call(
        functools.partial(_delta_kernel, heads),
        grid=(batch, groups, per_seq),
        in_specs=[qkv, qkv, qkv,
                  pl.BlockSpec((rows, bw), lambda b, h, t: (b * per_seq + t, zb0 + h)),
                  pl.BlockSpec((rows, LANES), lambda b, h, t: (b * per_seq + t, 0)),
                  pl.BlockSpec((2 * heads, rows), lambda b, h, t: (0, b * per_seq + t)),
                  pl.BlockSpec((1, HEAD_DIM), lambda b, h, t: (0, 0)),
                  pl.BlockSpec((None, w_block, w_cols), lambda b, h, t: (layer, w_idx(b, h, t), 0))],
        out_specs=[qkv, pl.BlockSpec((w_block, w_cols), lambda b, h, t: (w_idx(b, h, t), 0))],
        out_shape=[jax.ShapeDtypeStruct((m, width), BF16), jax.ShapeDtypeStruct((w_rows, w_cols), BF16)],
        scratch_shapes=[pltpu.VMEM((hp, HEAD_DIM, HEAD_DIM), F32)],
        compiler_params=_params("arbitrary", "arbitrary", "arbitrary"),
        name="gated_delta",
    )(q, k, v, proj, gates, gates_t, dn_norm.reshape(1, HEAD_DIM), w_src)


def _merge_kernel(ga_ref, gb_ref, pa_ref, ob_ref, pp_ref, dp_ref, y_ref):
    ya = _sigmoid(ga_ref[...].astype(F32)) * _dot(pa_ref[...], pp_ref[...].astype(BF16))
    yb = _sigmoid(gb_ref[...].astype(F32)) * _dot(ob_ref[...], dp_ref[...].astype(BF16))
    y_ref[...] = (ya + yb).astype(y_ref.dtype)


def _merge(proj, gate_col0, pa, ob, pool_proj, dn_proj, layer):
    m, kp = pa.shape
    d = pool_proj.shape[2]
    tm, tn = min(MATMUL_ROWS, m), min(MATMUL_COLS, d)
    nb = d // tn
    assert gate_col0 % tn == 0
    g0 = gate_col0 // tn
    return pl.pallas_call(
        _merge_kernel,
        grid=(nb, m // tm),
        in_specs=[pl.BlockSpec((tm, tn), lambda j, i: (i, g0 + j)),
                  pl.BlockSpec((tm, tn), lambda j, i: (i, g0 + nb + j)),
                  pl.BlockSpec((tm, kp), lambda j, i: (i, 0)),
                  pl.BlockSpec((tm, kp), lambda j, i: (i, 0)),
                  pl.BlockSpec((None, kp, tn), lambda j, i: (layer, 0, j)),
                  pl.BlockSpec((None, kp, tn), lambda j, i: (layer, 0, j))],
        out_specs=pl.BlockSpec((tm, tn), lambda j, i: (i, j)),
        out_shape=jax.ShapeDtypeStruct((m, d), BF16),
        compiler_params=_params("parallel", "arbitrary"),
        name="gated_merge",
    )(proj, proj, pa, ob, pool_proj, dn_proj)


FFN_CARRY = 8


def _ffn_up_kernel(per_seq, ragged, h_ref, wa_ref, wu0_ref, wu1_ref, cw_ref, cb_ref, wsrc_ref, o_ref, w16_ref,
                   a_ref):
    tm, tn = o_ref.shape
    half = tn // 2
    w16_ref[...] = wsrc_ref[...].astype(BF16)

    @pl.when(pl.program_id(1) % per_seq == 0)
    def _():
        a_ref[0:FFN_CARRY, :] = jnp.zeros((FFN_CARRY, tn), F32)

    def tile(width):
        cols = slice(0, width)
        h = h_ref[...]
        a_ref[FFN_CARRY:, cols] = _dot(h, wa_ref[:, cols])
        u = _dot(h, wu0_ref[...])
        if width > half:
            u = jnp.concatenate([u, _dot(h, wu1_ref[...])], axis=1)
        a = cb_ref[:, cols]
        for tap in range(FFN_CONV_TAPS):
            r0 = FFN_CARRY - (FFN_CONV_TAPS - 1) + tap
            a = a + a_ref[r0:r0 + tm, cols] * cw_ref[tap:tap + 1, cols]
        o_ref[:, cols] = (_silu(a) * u).astype(o_ref.dtype)
        a_ref[0:FFN_CARRY, cols] = a_ref[tm:tm + FFN_CARRY, cols]

    if ragged:
        last = pl.program_id(0) == pl.num_programs(0) - 1
        pl.when(jnp.logical_not(last))(lambda: tile(tn))
        pl.when(last)(lambda: tile(half))
    else:
        tile(tn)


def _ffn_up(h, w_up, layer, conv_w, conv_b, seq, w_src):
    m, d = h.shape
    n = w_up.shape[1] // 2
    tm, tn = min(MATMUL_ROWS, seq), MATMUL_COLS
    half = tn // 2
    assert n % half == 0
    nh = n // half
    nt = pl.cdiv(n, tn)
    per_seq = seq // tm
    pad = nt * tn - n
    conv_w = jnp.pad(conv_w, ((0, 0), (0, 0), (0, pad)))
    conv_b = jnp.pad(conv_b, ((0, 0), (0, pad))).reshape(conv_b.shape[0], 1, nt * tn)
    row_tiles = m // tm
    _, w_rows, w_cols = w_src.shape
    w_block, w_blocks = _ride_along_blocks(w_rows, nt * row_tiles)
    w_idx = lambda j, i: jnp.minimum(j * row_tiles + i, w_blocks - 1)
    return pl.pallas_call(
        functools.partial(_ffn_up_kernel, per_seq, n % tn != 0),
        grid=(nt, row_tiles),
        in_specs=[pl.BlockSpec((tm, d), lambda j, i: (i, 0)),
                  pl.BlockSpec((d, tn), lambda j, i: (0, j)),
                  pl.BlockSpec((d, half), lambda j, i: (0, nh + 2 * j)),
                  pl.BlockSpec((d, half), lambda j, i: (0, jnp.minimum(nh + 2 * j + 1, 2 * nh - 1))),
                  pl.BlockSpec((None, FFN_CONV_TAPS, tn), lambda j, i: (layer, 0, j)),
                  pl.BlockSpec((None, 1, tn), lambda j, i: (layer, 0, j)),
                  pl.BlockSpec((None, w_block, w_cols), lambda j, i: (layer, w_idx(j, i), 0))],
        out_specs=[pl.BlockSpec((tm, tn), lambda j, i: (i, j)),
                   pl.BlockSpec((w_block, w_cols), lambda j, i: (w_idx(j, i), 0))],
        out_shape=[jax.ShapeDtypeStruct((m, n), BF16), jax.ShapeDtypeStruct((w_rows, w_cols), BF16)],
        scratch_shapes=[pltpu.VMEM((tm + FFN_CARRY, tn), F32)],
        compiler_params=_params("arbitrary", "arbitrary"),
        name="ffn_up_conv_gate",
    )(h, w_up, w_up, w_up, conv_w, conv_b, w_src)


def kernel(x, c, w_ada, b_ada, ada_table, mix_pre_norm, w_in, pool_w, pool_scale, pool_proj,
           dn_conv, dn_a_log, dn_dt_bias, dn_norm, dn_proj, w_o, mix_post_norm,
           ffn_pre_norm, w_up, ffn_conv, ffn_conv_b, w_down, ffn_post_norm):
    batch, seq, d = x.shape
    depth = w_in.shape[0]
    m = batch * seq
    pool_width = pool_scale.shape[1]
    dn_width = dn_proj.shape[1]
    heads = dn_a_log.shape[1]
    c_qkv = pool_width
    c_z = c_qkv + 3 * dn_width
    c_ba = c_z + dn_width

    c8 = jnp.pad(c, ((0, SUBLANES - batch), (0, 0)))
    mod = _ada(c8, w_ada, b_ada, ada_table.reshape(depth, N_MOD * d))
    mod = mod[:, :batch].reshape(depth, batch, N_MOD, 1, d)

    xf = x.reshape(m, d)
    h = _norm_mod(xf, mix_pre_norm[0], mod[0, :, 1], mod[0, :, 0], seq)
    w_in_t = jnp.swapaxes(w_in, 1, 2)
    for l in range(depth):
        shift_f, scale_f, gate_m, gate_f = mod[l, :, 3], mod[l, :, 4], mod[l, :, 2], mod[l, :, 5]
        proj = _matmul_nt(h, w_in_t, l, c_ba + 2 * d, c_ba, 2 * heads, BF16, IN_PROJ_ROWS, MATMUL_COLS, "in_proj")
        gates = _ba(h, w_in_t, l, c_ba, dn_a_log[l], dn_dt_bias[l], heads)
        gates_t = gates[:, :2 * heads].T
        pa = _pool(proj, pool_w, l, pool_scale[l], seq)
        q, k, v = _qkv_conv(proj, dn_conv[l], dn_width, c_qkv, seq)
        ob, w_up16 = _delta(q, k, v, proj, c_z, gates, gates_t, dn_norm[l], batch, seq, heads, w_up, l)
        y = _merge(proj, c_ba, pa, ob, pool_proj, dn_proj, l)
        yo = _matmul(y, w_o, l, d, BF16, MATMUL_ROWS, MATMUL_COLS, "out_proj", rows_outer=True)
        xf, h = _resid_norm(yo, xf, mix_post_norm[l], gate_m, seq,
                            nxt=(ffn_pre_norm[l], scale_f, shift_f))
        act, w_down16 = _ffn_up(h, w_up16, l, ffn_conv, ffn_conv_b, seq, w_down)
        yd = _matmul(act, w_down16[None], 0, d, BF16, FFN_DOWN_ROWS, MATMUL_COLS, "ffn_down")
        if l + 1 < depth:
            xf, h = _resid_norm(yd, xf, ffn_post_norm[l], gate_f, seq,
                                nxt=(mix_pre_norm[l + 1], mod[l + 1, :, 1], mod[l + 1, :, 0]))
        else:
            xf = _resid_norm(yd, xf, ffn_post_norm[l], gate_f, seq)
    return xf.reshape(batch, seq, d)
```

```python
import functools

import jax
import jax.numpy as jnp
from jax import lax
from jax.experimental import pallas as pl
from jax.experimental.pallas import tpu as pltpu

F32 = jnp.float32
BF16 = jnp.bfloat16

EPS = 1e-6
POOL_WINDOWS = (2, 4, 8, 16)
HEAD_DIM = 128
DN_CONV_TAPS = 4
FFN_CONV_TAPS = 3
N_MOD = 6
CHUNK = 128
HALO = 16
VMEM_LIMIT_BYTES = 60 * 1024 * 1024
LANES = 128
SUBLANES = 8

ELEMENTWISE_ROWS = 256
CONV_ROWS = 128
GATE_ROWS = 512
MATMUL_ROWS, MATMUL_COLS = 1024, 512
IN_PROJ_ROWS = 2048
FFN_DOWN_ROWS = 512
DELTA_HEADS_PER_STEP = 16
ADA_COLS = 512


def _params(*semantics):
    return pltpu.CompilerParams(dimension_semantics=semantics,
                                vmem_limit_bytes=VMEM_LIMIT_BYTES)


def _dot(a, b):
    return jnp.dot(a, b, preferred_element_type=F32)


def _dot_nt(a, b):
    return lax.dot_general(a, b, (((1,), (1,)), ((), ())), preferred_element_type=F32)


def _sigmoid(x):
    return 0.5 + 0.5 * jnp.tanh(0.5 * x)


def _silu(x):
    return x * _sigmoid(x)


def _softplus(x):
    return jnp.maximum(x, 0.0) + jnp.log1p(jnp.exp(-jnp.abs(x)))


def _rms(x):
    return x * lax.rsqrt(jnp.mean(x * x, axis=-1, keepdims=True) + EPS)


def _ada_kernel(c_ref, w_ref, b_ref, t_ref, o_ref):
    a = _silu(c_ref[...]).astype(BF16)
    acc = _dot(a, w_ref[...].astype(BF16)) + b_ref[...]
    o_ref[...] = acc[None] + t_ref[...]


def _ada(c8, w_ada, b_ada, table):
    depth = table.shape[0]
    d, n = w_ada.shape
    tn = min(ADA_COLS, n)
    return pl.pallas_call(
        _ada_kernel,
        grid=(n // tn,),
        in_specs=[pl.BlockSpec((SUBLANES, d), lambda j: (0, 0)),
                  pl.BlockSpec((d, tn), lambda j: (0, j)),
                  pl.BlockSpec((1, tn), lambda j: (0, j)),
                  pl.BlockSpec((depth, 1, tn), lambda j: (0, 0, j))],
        out_specs=pl.BlockSpec((depth, SUBLANES, tn), lambda j: (0, 0, j)),
        out_shape=jax.ShapeDtypeStruct((depth, SUBLANES, n), F32),
        compiler_params=_params("arbitrary"),
        name="ada",
    )(c8, w_ada, b_ada.reshape(1, n), table.reshape(depth, 1, n))


def _norm_mod_kernel(x_ref, w_ref, sc_ref, sh_ref, h_ref):
    y = _rms(x_ref[...]) * w_ref[...]
    h_ref[...] = (y * (1.0 + sc_ref[0]) + sh_ref[0]).astype(h_ref.dtype)


def _norm_mod(x, w, scale, shift, seq):
    m, d = x.shape
    tm = min(ELEMENTWISE_ROWS, seq)
    per_seq = seq // tm
    row = pl.BlockSpec((tm, d), lambda i: (i, 0))
    vec = pl.BlockSpec((1, d), lambda i: (0, 0))
    bvec = pl.BlockSpec((1, 1, d), lambda i: (i // per_seq, 0, 0))
    return pl.pallas_call(
        _norm_mod_kernel,
        grid=(m // tm,),
        in_specs=[row, vec, bvec, bvec],
        out_specs=row,
        out_shape=jax.ShapeDtypeStruct((m, d), BF16),
        compiler_params=_params("parallel"),
        name="norm_mod",
    )(x, w.reshape(1, d), scale, shift)


def _resid_norm_kernel(y_ref, x_ref, pw_ref, g_ref, nw_ref, sc_ref, sh_ref, xo_ref, h_ref):
    xn = x_ref[...] + g_ref[0] * (_rms(y_ref[...].astype(F32)) * pw_ref[...])
    xo_ref[...] = xn
    h = _rms(xn) * nw_ref[...]
    h_ref[...] = (h * (1.0 + sc_ref[0]) + sh_ref[0]).astype(h_ref.dtype)


def _resid_kernel(y_ref, x_ref, pw_ref, g_ref, xo_ref):
    xo_ref[...] = x_ref[...] + g_ref[0] * (_rms(y_ref[...].astype(F32)) * pw_ref[...])


def _resid_norm(y, x, post_w, gate, seq, nxt=None):
    m, d = x.shape
    tm = min(ELEMENTWISE_ROWS, seq)
    per_seq = seq // tm
    row = pl.BlockSpec((tm, d), lambda i: (i, 0))
    vec = pl.BlockSpec((1, d), lambda i: (0, 0))
    bvec = pl.BlockSpec((1, 1, d), lambda i: (i // per_seq, 0, 0))
    if nxt is None:
        return pl.pallas_call(
            _resid_kernel,
            grid=(m // tm,),
            in_specs=[row, row, vec, bvec],
            out_specs=row,
            out_shape=jax.ShapeDtypeStruct((m, d), F32),
            compiler_params=_params("parallel"),
            name="resid",
        )(y, x, post_w.reshape(1, d), gate)
    norm_w, scale, shift = nxt
    return pl.pallas_call(
        _resid_norm_kernel,
        grid=(m // tm,),
        in_specs=[row, row, vec, bvec, vec, bvec, bvec],
        out_specs=[row, row],
        out_shape=[jax.ShapeDtypeStruct((m, d), F32), jax.ShapeDtypeStruct((m, d), BF16)],
        compiler_params=_params("parallel"),
        name="resid_norm",
    )(y, x, post_w.reshape(1, d), gate, norm_w.reshape(1, d), scale, shift)


def _mm_kernel(a_ref, w_ref, o_ref):
    o_ref[...] = _dot(a_ref[...], w_ref[...].astype(BF16)).astype(o_ref.dtype)


def _matmul(a, w, layer, n, out_dtype, tm, tn, name, rows_outer=False):
    m, k = a.shape
    tm, tn = min(tm, m), min(tn, n)
    if rows_outer:
        grid, ij = (m // tm, n // tn), lambda i, j: (i, j)
    else:
        grid, ij = (n // tn, m // tm), lambda j, i: (i, j)
    return pl.pallas_call(
        _mm_kernel,
        grid=grid,
        in_specs=[pl.BlockSpec((tm, k), lambda *g: (ij(*g)[0], 0)),
                  pl.BlockSpec((None, k, tn), lambda *g: (layer, 0, ij(*g)[1]))],
        out_specs=pl.BlockSpec((tm, tn), lambda *g: ij(*g)),
        out_shape=jax.ShapeDtypeStruct((m, n), out_dtype),
        compiler_params=_params("parallel", "arbitrary"),
        name=name,
    )(a, w)


def _mm_nt_kernel(a_ref, wt_ref, o_ref):
    o_ref[...] = _dot_nt(a_ref[...], wt_ref[0].astype(BF16)).astype(o_ref.dtype)


def _matmul_nt(a, wt, layer, n, skip_at, skip, out_dtype, tm, tn, name):
    m, k = a.shape
    tm, tn = min(tm, m), min(tn, n)
    sublanes = 8 * 4 // wt.dtype.itemsize
    assert skip % sublanes == 0 and n % tn == 0 and skip_at % tn == 0
    first_after = skip_at // tn
    row = lambda j: pl.multiple_of(j * tn + jnp.where(j >= first_after, skip, 0), sublanes)
    return pl.pallas_call(
        _mm_nt_kernel,
        grid=(m // tm, n // tn),
        in_specs=[pl.BlockSpec((tm, k), lambda i, j: (i, 0)),
                  pl.BlockSpec((pl.Element(1), pl.Element(tn), pl.Element(k)),
                               lambda i, j: (layer, row(j), 0))],
        out_specs=pl.BlockSpec((tm, tn), lambda i, j: (i, j)),
        out_shape=jax.ShapeDtypeStruct((m, n), out_dtype),
        compiler_params=_params("parallel", "arbitrary"),
        name=name,
    )(a, wt)


def _ba_kernel(heads, h_ref, w_ref, alog_ref, dtb_ref, o_ref):
    p = _dot_nt(h_ref[...], w_ref[...].astype(BF16))
    lane = lax.broadcasted_iota(jnp.int32, p.shape, 1)
    is_g = (lane >= heads) & (lane < 2 * heads)
    g = jnp.where(is_g, -jnp.exp(alog_ref[...]) * _softplus(p + dtb_ref[...]), 0.0)
    pos = lax.broadcasted_iota(jnp.int32, p.shape, 0) % CHUNK
    shift = 1
    while shift < CHUNK:
        g = g + jnp.where(pos >= shift, pltpu.roll(g, shift, axis=0), 0.0)
        shift *= 2
    o_ref[...] = jnp.where(is_g, g, _sigmoid(p))


def _ba(h, wt, layer, row0, a_log, dt_bias, heads):
    m, k = h.shape
    tm = min(GATE_ROWS, m)
    assert row0 % LANES == 0 and 2 * heads <= LANES
    pad = lambda v: jnp.zeros((1, LANES), F32).at[0, heads:2 * heads].set(v)
    return pl.pallas_call(
        functools.partial(_ba_kernel, heads),
        grid=(m // tm,),
        in_specs=[pl.BlockSpec((tm, k), lambda i: (i, 0)),
                  pl.BlockSpec((None, LANES, k), lambda i: (layer, row0 // LANES, 0)),
                  pl.BlockSpec((1, LANES), lambda i: (0, 0)),
                  pl.BlockSpec((1, LANES), lambda i: (0, 0))],
        out_specs=pl.BlockSpec((tm, LANES), lambda i: (i, 0)),
        out_shape=jax.ShapeDtypeStruct((m, LANES), F32),
        compiler_params=_params("parallel"),
        name="dn_gates",
    )(h, wt, pad(a_log), pad(dt_bias))


def _band(tm, lo, hi):
    ri = lax.broadcasted_iota(jnp.int32, (tm, tm + HALO), 0)
    ci = lax.broadcasted_iota(jnp.int32, (tm, tm + HALO), 1)
    back = ri + HALO - ci
    return jnp.where((back >= lo) & (back <= hi), 1.0, 0.0).astype(BF16)


def _pool_kernel(per_seq, u_ref, halo_ref, pw_ref, sc_ref, o_ref, ext_ref):
    tm, width = u_ref.shape
    group = width // len(POOL_WINDOWS)
    t = pl.program_id(0) % per_seq
    ext_ref[0:HALO, :] = jnp.where(t == 0, jnp.zeros(halo_ref.shape, halo_ref.dtype), halo_ref[...])
    ext_ref[HALO:, :] = u_ref[...]
    pos = t * tm + lax.broadcasted_iota(jnp.int32, (tm, 1), 0)
    for gi, win in enumerate(POOL_WINDOWS):
        cols = slice(gi * group, (gi + 1) * group)
        window_sum = _dot(_band(tm, 0, win - 1), ext_ref[:, cols])
        inv_count = 1.0 / jnp.minimum(pos + 1, win).astype(F32)
        pa = window_sum * inv_count - u_ref[:, cols].astype(F32)
        r = _dot(pa.astype(BF16), pw_ref[gi].astype(BF16)) * sc_ref[:, cols]
        o_ref[:, cols] = r.astype(o_ref.dtype)


def _pool(proj, pool_w, layer, pool_scale, seq):
    m = proj.shape[0]
    _, groups, group, _ = pool_w.shape
    width = groups * group
    tm = min(ELEMENTWISE_ROWS, seq)
    per_seq = seq // tm
    return pl.pallas_call(
        functools.partial(_pool_kernel, per_seq),
        grid=(m // tm,),
        in_specs=[pl.BlockSpec((tm, width), lambda i: (i, 0)),
                  pl.BlockSpec((HALO, width), lambda i: (jnp.maximum(i * (tm // HALO) - 1, 0), 0)),
                  pl.BlockSpec((None, groups, group, group), lambda i: (layer, 0, 0, 0)),
                  pl.BlockSpec((1, width), lambda i: (0, 0))],
        out_specs=pl.BlockSpec((tm, width), lambda i: (i, 0)),
        out_shape=jax.ShapeDtypeStruct((m, width), BF16),
        scratch_shapes=[pltpu.VMEM((tm + HALO, width), BF16)],
        compiler_params=_params("parallel"),
        name="pool_mixer",
    )(proj, proj, pool_w, pool_scale.reshape(1, width))


def _qkv_kernel(per_seq, q_ref, k_ref, v_ref, hq_ref, hk_ref, hv_ref, wq_ref, wk_ref, wv_ref,
                qo_ref, ko_ref, vo_ref, ext_ref):
    tm, width = q_ref.shape
    first = pl.program_id(0) % per_seq == 0
    taps = DN_CONV_TAPS
    shifts = jnp.concatenate([_band(tm, s, s) for s in range(1, taps)], axis=0)
    chunk = 2 * HEAD_DIM
    for x_ref, halo_ref, w_ref, o_ref, norm in (
            (q_ref, hq_ref, wq_ref, qo_ref, HEAD_DIM ** -0.5),
            (k_ref, hk_ref, wk_ref, ko_ref, 1.0),
            (v_ref, hv_ref, wv_ref, vo_ref, None)):
        ext_ref[0:HALO, :] = jnp.where(first, jnp.zeros(halo_ref.shape, halo_ref.dtype), halo_ref[...])
        ext_ref[HALO:, :] = x_ref[...]
        for c0 in range(0, width, chunk):
            wide = slice(c0, c0 + chunk)
            back = _dot(shifts, ext_ref[:, wide])
            y = x_ref[:, wide].astype(F32) * w_ref[taps - 1:taps, wide]
            for s in range(1, taps):
                y = y + back[(s - 1) * tm:s * tm] * w_ref[taps - 1 - s:taps - s, wide]
            y = _silu(y)
            for c in range(0, chunk, HEAD_DIM):
                yh = y[:, c:c + HEAD_DIM]
                if norm is not None:
                    yh = yh * (lax.rsqrt(jnp.sum(yh * yh, axis=-1, keepdims=True) + EPS) * norm)
                o_ref[:, c0 + c:c0 + c + HEAD_DIM] = yh.astype(o_ref.dtype)


def _qkv_conv(proj, dn_conv, width, col0, seq):
    m = proj.shape[0]
    tm = min(CONV_ROWS, seq)
    per_seq = seq // tm
    b0 = col0 // width
    rows = [pl.BlockSpec((tm, width), functools.partial(lambda s, i: (i, b0 + s), s)) for s in range(3)]
    halos = [pl.BlockSpec((HALO, width),
                          functools.partial(lambda s, i: (jnp.maximum(i * (tm // HALO) - 1, 0), b0 + s), s))
             for s in range(3)]
    taps = [pl.BlockSpec((DN_CONV_TAPS, width), functools.partial(lambda s, i: (0, s), s)) for s in range(3)]
    out = pl.BlockSpec((tm, width), lambda i: (i, 0))
    shp = jax.ShapeDtypeStruct((m, width), BF16)
    return pl.pallas_call(
        functools.partial(_qkv_kernel, per_seq),
        grid=(m // tm,),
        in_specs=rows + halos + taps,
        out_specs=[out, out, out],
        out_shape=[shp, shp, shp],
        scratch_shapes=[pltpu.VMEM((tm + HALO, width), BF16)],
        compiler_params=_params("parallel"),
        name="dn_qkv_conv",
    )(proj, proj, proj, proj, proj, proj, dn_conv, dn_conv, dn_conv)


def _unit_lower_inverses(lows):
    n = lows[0].shape[0]
    ri = lax.broadcasted_iota(jnp.int32, (n, n), 0)
    ci = lax.broadcasted_iota(jnp.int32, (n, n), 1)
    same = lambda bits: jnp.right_shift(ri, bits) == jnp.right_shift(ci, bits)
    eye = jnp.where(ri == ci, 1.0, 0.0)
    invs = [eye - jnp.where(same(1), low, 0.0) for low in lows]
    bits = 1
    while (1 << bits) < n:
        mask = same(bits + 1) & jnp.logical_not(same(bits))
        inv16 = [inv.astype(BF16) for inv in invs]
        right = [_dot(jnp.where(mask, low, 0.0).astype(BF16), i16) for low, i16 in zip(lows, inv16)]
        invs = [inv - _dot(i16, r.astype(BF16)) for inv, i16, r in zip(invs, inv16, right)]
        bits += 1
    return invs


def _ride_along_blocks(rows, steps):
    block = next(b for b in range(16, rows + 1, 16) if rows % b == 0 and rows // b <= steps)
    return block, rows // block


def _delta_kernel(heads, q_ref, k_ref, v_ref, z_ref, g_ref, gt_ref, nw_ref, wsrc_ref, o_ref, w16_ref,
                  state_ref):
    rows, width = q_ref.shape
    hp = width // HEAD_DIM
    c = CHUNK
    w16_ref[...] = wsrc_ref[...].astype(BF16)

    @pl.when(pl.program_id(2) == 0)
    def _():
        state_ref[...] = jnp.zeros_like(state_ref)

    gates = g_ref[...]
    lane = lax.broadcasted_iota(jnp.int32, gates.shape, 1)
    ri = lax.broadcasted_iota(jnp.int32, (c, c), 0)
    ci = lax.broadcasted_iota(jnp.int32, (c, c), 1)
    causal = ri >= ci
    strict = ri > ci

    work = []
    for hh in range(hp):
        head = pl.program_id(1) * hp + hh
        cols = slice(hh * HEAD_DIM, (hh + 1) * HEAD_DIM)
        beta_col = jnp.sum(jnp.where(lane == head, gates, 0.0), axis=-1, keepdims=True)
        gc_col = jnp.sum(jnp.where(lane == heads + head, gates, 0.0), axis=-1, keepdims=True)
        gc_row = gt_ref[pl.ds(heads + head, 1), :]
        for r0 in range(0, rows, c):
            rsl = slice(r0, r0 + c)
            q = q_ref[rsl, cols]
            k16 = k_ref[rsl, cols]
            k = k16.astype(F32)
            beta = beta_col[rsl]
            gcc = gc_col[rsl]
            gcr = gc_row[:, rsl]
            g_last = gcr[:, c - 1:c]
            decay = jnp.exp(jnp.where(causal, gcc - gcr, -jnp.inf))
            e_g = jnp.exp(gcc)
            kb = k * beta
            kq_k = _dot_nt(jnp.concatenate([kb.astype(BF16), q], axis=0), k16)
            low = jnp.where(strict, kq_k[:c] * decay, 0.0)
            qk = jnp.where(causal, kq_k[c:] * decay, 0.0)
            work.append(dict(
                hh=hh, rsl=rsl, cols=cols, low=low,
                vb_kbg16=jnp.concatenate([(v_ref[rsl, cols].astype(F32) * beta).astype(BF16),
                                          (kb * e_g).astype(BF16)], axis=1),
                qd16=(q.astype(F32) * e_g).astype(BF16),
                qk_kdt16=jnp.concatenate([qk.astype(BF16),
                                          (k * jnp.exp(g_last - gcc)).T.astype(BF16)], axis=0),
                g_end=jnp.exp(g_last)))
    for item, t_inv in zip(work, _unit_lower_inverses([item["low"] for item in work])):
        uw = _dot(t_inv.astype(BF16), item["vb_kbg16"])
        item["u"] = uw[:, :HEAD_DIM]
        item["w_qd16"] = jnp.concatenate([uw[:, HEAD_DIM:].astype(BF16), item["qd16"]], axis=0)

    for hh in range(hp):
        state = state_ref[hh]
        for item in work:
            if item["hh"] != hh:
                continue
            from_state = _dot(item["w_qd16"], state.astype(BF16))
            vn16 = (item["u"] - from_state[:c]).astype(BF16)
            from_new = _dot(item["qk_kdt16"], vn16)
            item["o"] = from_state[c:] + from_new[:c]
            state = state * item["g_end"] + from_new[c:]
        state_ref[hh] = state

    for item in work:
        rsl, cols = item["rsl"], item["cols"]
        z = z_ref[rsl, cols].astype(F32)
        o_ref[rsl, cols] = (_rms(item["o"]) * nw_ref[...] * _silu(z)).astype(o_ref.dtype)


def _delta(q, k, v, proj, z_col0, gates, gates_t, dn_norm, batch, seq, heads, w_src, layer):
    m, width = q.shape
    hp = min(DELTA_HEADS_PER_STEP, heads)
    rows = min(CHUNK, seq)
    per_seq = seq // rows
    groups = heads // hp
    bw = hp * HEAD_DIM
    zb0 = z_col0 // bw
    _, w_rows, w_cols = w_src.shape
    w_block, w_blocks = _ride_along_blocks(w_rows, batch * groups * per_seq)
    w_idx = lambda b, h, t: jnp.minimum((b * groups + h) * per_seq + t, w_blocks - 1)
    qkv = pl.BlockSpec((rows, bw), lambda b, h, t: (b * per_seq + t, h))
    return pl.pallas_call(
        functools.partial(_delta_kernel, heads),
        grid=(batch, groups, per_seq),
        in_specs=[qkv, qkv, qkv,
                  pl.BlockSpec((rows, bw), lambda b, h, t: (b * per_seq + t, zb0 + h)),
                  pl.BlockSpec((rows, LANES), lambda b, h, t: (b * per_seq + t, 0)),
                  pl.BlockSpec((2 * heads, rows), lambda b, h, t: (0, b * per_seq + t)),
                  pl.BlockSpec((1, HEAD_DIM), lambda b, h, t: (0, 0)),
                  pl.BlockSpec((None, w_block, w_cols), lambda b, h, t: (layer, w_idx(b, h, t), 0))],
        out_specs=[qkv, pl.BlockSpec((w_block, w_cols), lambda b, h, t: (w_idx(b, h, t), 0))],
        out_shape=[jax.ShapeDtypeStruct((m, width), BF16), jax.ShapeDtypeStruct((w_rows, w_cols), BF16)],
        scratch_shapes=[pltpu.VMEM((hp, HEAD_DIM, HEAD_DIM), F32)],
        compiler_params=_params("arbitrary", "arbitrary", "arbitrary"),
        name="gated_delta",
    )(q, k, v, proj, gates, gates_t, dn_norm.reshape(1, HEAD_DIM), w_src)


def _merge_kernel(ga_ref, gb_ref, pa_ref, ob_ref, pp_ref, dp_ref, y_ref):
    ya = _sigmoid(ga_ref[...].astype(F32)) * _dot(pa_ref[...], pp_ref[...].astype(BF16))
    yb = _sigmoid(gb_ref[...].astype(F32)) * _dot(ob_ref[...], dp_ref[...].astype(BF16))
    y_ref[...] = (ya + yb).astype(y_ref.dtype)


def _merge(proj, gate_col0, pa, ob, pool_proj, dn_proj, layer):
    m, kp = pa.shape
    d = pool_proj.shape[2]
    tm, tn = min(MATMUL_ROWS, m), min(MATMUL_COLS, d)
    nb = d // tn
    assert gate_col0 % tn == 0
    g0 = gate_col0 // tn
    return pl.pallas_call(
        _merge_kernel,
        grid=(nb, m // tm),
        in_specs=[pl.BlockSpec((tm, tn), lambda j, i: (i, g0 + j)),
                  pl.BlockSpec((tm, tn), lambda j, i: (i, g0 + nb + j)),
                  pl.BlockSpec((tm, kp), lambda j, i: (i, 0)),
                  pl.BlockSpec((tm, kp), lambda j, i: (i, 0)),
                  pl.BlockSpec((None, kp, tn), lambda j, i: (layer, 0, j)),
                  pl.BlockSpec((None, kp, tn), lambda j, i: (layer, 0, j))],
        out_specs=pl.BlockSpec((tm, tn), lambda j, i: (i, j)),
        out_shape=jax.ShapeDtypeStruct((m, d), BF16),
        compiler_params=_params("parallel", "arbitrary"),
        name="gated_merge",
    )(proj, proj, pa, ob, pool_proj, dn_proj)


FFN_CARRY = 8


def _ffn_up_kernel(per_seq, ragged, h_ref, wa_ref, wu0_ref, wu1_ref, cw_ref, cb_ref, wsrc_ref, o_ref, w16_ref,
                   a_ref):
    tm, tn = o_ref.shape
    half = tn // 2
    w16_ref[...] = wsrc_ref[...].astype(BF16)

    @pl.when(pl.program_id(1) % per_seq == 0)
    def _():
        a_ref[0:FFN_CARRY, :] = jnp.zeros((FFN_CARRY, tn), F32)

    def tile(width):
        cols = slice(0, width)
        h = h_ref[...]
        a_ref[FFN_CARRY:, cols] = _dot(h, wa_ref[:, cols])
        u = _dot(h, wu0_ref[...])
        if width > half:
            u = jnp.concatenate([u, _dot(h, wu1_ref[...])], axis=1)
        a = cb_ref[:, cols]
        for tap in range(FFN_CONV_TAPS):
            r0 = FFN_CARRY - (FFN_CONV_TAPS - 1) + tap
            a = a + a_ref[r0:r0 + tm, cols] * cw_ref[tap:tap + 1, cols]
        o_ref[:, cols] = (_silu(a) * u).astype(o_ref.dtype)
        a_ref[0:FFN_CARRY, cols] = a_ref[tm:tm + FFN_CARRY, cols]

    if ragged:
        last = pl.program_id(0) == pl.num_programs(0) - 1
        pl.when(jnp.logical_not(last))(lambda: tile(tn))
        pl.when(last)(lambda: tile(half))
    else:
        tile(tn)


def _ffn_up(h, w_up, layer, conv_w, conv_b, seq, w_src):
    m, d = h.shape
    n = w_up.shape[1] // 2
    tm, tn = min(MATMUL_ROWS, seq), MATMUL_COLS
    half = tn // 2
    assert n % half == 0
    nh = n // half
    nt = pl.cdiv(n, tn)
    per_seq = seq // tm
    pad = nt * tn - n
    conv_w = jnp.pad(conv_w, ((0, 0), (0, 0), (0, pad)))
    conv_b = jnp.pad(conv_b, ((0, 0), (0, pad))).reshape(conv_b.shape[0], 1, nt * tn)
    row_tiles = m // tm
    _, w_rows, w_cols = w_src.shape
    w_block, w_blocks = _ride_along_blocks(w_rows, nt * row_tiles)
    w_idx = lambda j, i: jnp.minimum(j * row_tiles + i, w_blocks - 1)
    return pl.pallas_call(
        functools.partial(_ffn_up_kernel, per_seq, n % tn != 0),
        grid=(nt, row_tiles),
        in_specs=[pl.BlockSpec((tm, d), lambda j, i: (i, 0)),
                  pl.BlockSpec((d, tn), lambda j, i: (0, j)),
                  pl.BlockSpec((d, half), lambda j, i: (0, nh + 2 * j)),
                  pl.BlockSpec((d, half), lambda j, i: (0, jnp.minimum(nh + 2 * j + 1, 2 * nh - 1))),
                  pl.BlockSpec((None, FFN_CONV_TAPS, tn), lambda j, i: (layer, 0, j)),
                  pl.BlockSpec((None, 1, tn), lambda j, i: (layer, 0, j)),
                  pl.BlockSpec((None, w_block, w_cols), lambda j, i: (layer, w_idx(j, i), 0))],
        out_specs=[pl.BlockSpec((tm, tn), lambda j, i: (i, j)),
                   pl.BlockSpec((w_block, w_cols), lambda j, i: (w_idx(j, i), 0))],
        out_shape=[jax.ShapeDtypeStruct((m, n), BF16), jax.ShapeDtypeStruct((w_rows, w_cols), BF16)],
        scratch_shapes=[pltpu.VMEM((tm + FFN_CARRY, tn), F32)],
        compiler_params=_params("arbitrary", "arbitrary"),
        name="ffn_up_conv_gate",
    )(h, w_up, w_up, w_up, conv_w, conv_b, w_src)


def kernel(x, c, w_ada, b_ada, ada_table, mix_pre_norm, w_in, pool_w, pool_scale, pool_proj,
           dn_conv, dn_a_log, dn_dt_bias, dn_norm, dn_proj, w_o, mix_post_norm,
           ffn_pre_norm, w_up, ffn_conv, ffn_conv_b, w_down, ffn_post_norm):
    batch, seq, d = x.shape
    depth = w_in.shape[0]
    m = batch * seq
    pool_width = pool_scale.shape[1]
    dn_width = dn_proj.shape[1]
    heads = dn_a_log.shape[1]
    c_qkv = pool_width
    c_z = c_qkv + 3 * dn_width
    c_ba = c_z + dn_width

    c8 = jnp.pad(c, ((0, SUBLANES - batch), (0, 0)))
    mod = _ada(c8, w_ada, b_ada, ada_table.reshape(depth, N_MOD * d))
    mod = mod[:, :batch].reshape(depth, batch, N_MOD, 1, d)

    xf = x.reshape(m, d)
    h = _norm_mod(xf, mix_pre_norm[0], mod[0, :, 1], mod[0, :, 0], seq)
    w_in_t = jnp.swapaxes(w_in, 1, 2)
    for l in range(depth):
        shift_f, scale_f, gate_m, gate_f = mod[l, :, 3], mod[l, :, 4], mod[l, :, 2], mod[l, :, 5]
        proj = _matmul_nt(h, w_in_t, l, c_ba + 2 * d, c_ba, 2 * heads, BF16, IN_PROJ_ROWS, MATMUL_COLS, "in_proj")
        gates = _ba(h, w_in_t, l, c_ba, dn_a_log[l], dn_dt_bias[l], heads)
        gates_t = gates[:, :2 * heads].T
        pa = _pool(proj, pool_w, l, pool_scale[l], seq)
        q, k, v = _qkv_conv(proj, dn_conv[l], dn_width, c_qkv, seq)
        ob, w_up16 = _delta(q, k, v, proj, c_z, gates, gates_t, dn_norm[l], batch, seq, heads, w_up, l)
        y = _merge(proj, c_ba, pa, ob, pool_proj, dn_proj, l)
        yo = _matmul(y, w_o, l, d, BF16, IN_PROJ_ROWS, MATMUL_COLS, "out_proj", rows_outer=True)
        xf, h = _resid_norm(yo, xf, mix_post_norm[l], gate_m, seq,
                            nxt=(ffn_pre_norm[l], scale_f, shift_f))
        act, w_down16 = _ffn_up(h, w_up16, l, ffn_conv, ffn_conv_b, seq, w_down)
        yd = _matmul(act, w_down16[None], 0, d, BF16, FFN_DOWN_ROWS, MATMUL_COLS, "ffn_down")
        if l + 1 < depth:
            xf, h = _resid_norm(yd, xf, ffn_post_norm[l], gate_f, seq,
                                nxt=(mix_pre_norm[l + 1], mod[l + 1, :, 1], mod[l + 1, :, 0]))
        else:
            xf = _resid_norm(yd, xf, ffn_post_norm[l], gate_f, seq)
    return xf.reshape(batch, seq, d)
```

```python
import functools

import jax
import jax.numpy as jnp
from jax import lax
from jax.experimental import pallas as pl
from jax.experimental.pallas import tpu as pltpu

F32 = jnp.float32
BF16 = jnp.bfloat16

EPS = 1e-6
POOL_WINDOWS = (2, 4, 8, 16)
HEAD_DIM = 128
DN_CONV_TAPS = 4
FFN_CONV_TAPS = 3
N_MOD = 6
CHUNK = 128
HALO = 16
VMEM_LIMIT_BYTES = 60 * 1024 * 1024
LANES = 128
SUBLANES = 8

ELEMENTWISE_ROWS = 512
CONV_ROWS = 128
GATE_ROWS = 512
MATMUL_ROWS, MATMUL_COLS = 1024, 512
IN_PROJ_ROWS = 2048
FFN_DOWN_ROWS = 512
DELTA_HEADS_PER_STEP = 16
ADA_COLS = 1024


def _params(*semantics):
    return pltpu.CompilerParams(dimension_semantics=semantics,
                                vmem_limit_bytes=VMEM_LIMIT_BYTES)


def _dot(a, b):
    return jnp.dot(a, b, preferred_element_type=F32)


def _dot_nt(a, b):
    return lax.dot_general(a, b, (((1,), (1,)), ((), ())), preferred_element_type=F32)


def _sigmoid(x):
    return 0.5 + 0.5 * jnp.tanh(0.5 * x)


def _silu(x):
    return x * _sigmoid(x)


def _softplus(x):
    return jnp.maximum(x, 0.0) + jnp.log1p(jnp.exp(-jnp.abs(x)))


def _rms(x):
    return x * lax.rsqrt(jnp.mean(x * x, axis=-1, keepdims=True) + EPS)


def _ada_kernel(c_ref, w_ref, b_ref, t_ref, o_ref):
    a = _silu(c_ref[...]).astype(BF16)
    acc = _dot(a, w_ref[...].astype(BF16)) + b_ref[...]
    o_ref[...] = acc[None] + t_ref[...]


def _ada(c8, w_ada, b_ada, table):
    depth = table.shape[0]
    d, n = w_ada.shape
    tn = min(ADA_COLS, n)
    return pl.pallas_call(
        _ada_kernel,
        grid=(n // tn,),
        in_specs=[pl.BlockSpec((SUBLANES, d), lambda j: (0, 0)),
                  pl.BlockSpec((d, tn), lambda j: (0, j)),
                  pl.BlockSpec((1, tn), lambda j: (0, j)),
                  pl.BlockSpec((depth, 1, tn), lambda j: (0, 0, j))],
        out_specs=pl.BlockSpec((depth, SUBLANES, tn), lambda j: (0, 0, j)),
        out_shape=jax.ShapeDtypeStruct((depth, SUBLANES, n), F32),
        compiler_params=_params("arbitrary"),
        name="ada",
    )(c8, w_ada, b_ada.reshape(1, n), table.reshape(depth, 1, n))


def _norm_mod_kernel(x_ref, w_ref, sc_ref, sh_ref, h_ref):
    y = _rms(x_ref[...]) * w_ref[...]
    h_ref[...] = (y * (1.0 + sc_ref[0]) + sh_ref[0]).astype(h_ref.dtype)


def _norm_mod(x, w, scale, shift, seq):
    m, d = x.shape
    tm = min(ELEMENTWISE_ROWS, seq)
    per_seq = seq // tm
    row = pl.BlockSpec((tm, d), lambda i: (i, 0))
    vec = pl.BlockSpec((1, d), lambda i: (0, 0))
    bvec = pl.BlockSpec((1, 1, d), lambda i: (i // per_seq, 0, 0))
    return pl.pallas_call(
        _norm_mod_kernel,
        grid=(m // tm,),
        in_specs=[row, vec, bvec, bvec],
        out_specs=row,
        out_shape=jax.ShapeDtypeStruct((m, d), BF16),
        compiler_params=_params("parallel"),
        name="norm_mod",
    )(x, w.reshape(1, d), scale, shift)


def _resid_norm_kernel(y_ref, x_ref, pw_ref, g_ref, nw_ref, sc_ref, sh_ref, xo_ref, h_ref):
    xn = x_ref[...] + g_ref[0] * (_rms(y_ref[...].astype(F32)) * pw_ref[...])
    xo_ref[...] = xn
    h = _rms(xn) * nw_ref[...]
    h_ref[...] = (h * (1.0 + sc_ref[0]) + sh_ref[0]).astype(h_ref.dtype)


def _resid_kernel(y_ref, x_ref, pw_ref, g_ref, xo_ref):
    xo_ref[...] = x_ref[...] + g_ref[0] * (_rms(y_ref[...].astype(F32)) * pw_ref[...])


def _resid_norm(y, x, post_w, gate, seq, nxt=None):
    m, d = x.shape
    tm = min(ELEMENTWISE_ROWS, seq)
    per_seq = seq // tm
    row = pl.BlockSpec((tm, d), lambda i: (i, 0))
    vec = pl.BlockSpec((1, d), lambda i: (0, 0))
    bvec = pl.BlockSpec((1, 1, d), lambda i: (i // per_seq, 0, 0))
    if nxt is None:
        return pl.pallas_call(
            _resid_kernel,
            grid=(m // tm,),
            in_specs=[row, row, vec, bvec],
            out_specs=row,
            out_shape=jax.ShapeDtypeStruct((m, d), F32),
            compiler_params=_params("parallel"),
            name="resid",
        )(y, x, post_w.reshape(1, d), gate)
    norm_w, scale, shift = nxt
    return pl.pallas_call(
        _resid_norm_kernel,
        grid=(m // tm,),
        in_specs=[row, row, vec, bvec, vec, bvec, bvec],
        out_specs=[row, row],
        out_shape=[jax.ShapeDtypeStruct((m, d), F32), jax.ShapeDtypeStruct((m, d), BF16)],
        compiler_params=_params("parallel"),
        name="resid_norm",
    )(y, x, post_w.reshape(1, d), gate, norm_w.reshape(1, d), scale, shift)


def _mm_kernel(a_ref, w_ref, o_ref):
    o_ref[...] = _dot(a_ref[...], w_ref[...].astype(BF16)).astype(o_ref.dtype)


def _matmul(a, w, layer, n, out_dtype, tm, tn, name, rows_outer=False):
    m, k = a.shape
    tm, tn = min(tm, m), min(tn, n)
    if rows_outer:
        grid, ij = (m // tm, n // tn), lambda i, j: (i, j)
    else:
        grid, ij = (n // tn, m // tm), lambda j, i: (i, j)
    return pl.pallas_call(
        _mm_kernel,
        grid=grid,
        in_specs=[pl.BlockSpec((tm, k), lambda *g: (ij(*g)[0], 0)),
                  pl.BlockSpec((None, k, tn), lambda *g: (layer, 0, ij(*g)[1]))],
        out_specs=pl.BlockSpec((tm, tn), lambda *g: ij(*g)),
        out_shape=jax.ShapeDtypeStruct((m, n), out_dtype),
        compiler_params=_params("parallel", "arbitrary"),
        name=name,
    )(a, w)


def _mm_nt_kernel(a_ref, wt_ref, o_ref):
    o_ref[...] = _dot_nt(a_ref[...], wt_ref[0].astype(BF16)).astype(o_ref.dtype)


def _matmul_nt(a, wt, layer, n, skip_at, skip, out_dtype, tm, tn, name):
    m, k = a.shape
    tm, tn = min(tm, m), min(tn, n)
    sublanes = 8 * 4 // wt.dtype.itemsize
    assert skip % sublanes == 0 and n % tn == 0 and skip_at % tn == 0
    first_after = skip_at // tn
    row = lambda j: pl.multiple_of(j * tn + jnp.where(j >= first_after, skip, 0), sublanes)
    return pl.pallas_call(
        _mm_nt_kernel,
        grid=(m // tm, n // tn),
        in_specs=[pl.BlockSpec((tm, k), lambda i, j: (i, 0)),
                  pl.BlockSpec((pl.Element(1), pl.Element(tn), pl.Element(k)),
                               lambda i, j: (layer, row(j), 0))],
        out_specs=pl.BlockSpec((tm, tn), lambda i, j: (i, j)),
        out_shape=jax.ShapeDtypeStruct((m, n), out_dtype),
        compiler_params=_params("parallel", "arbitrary"),
        name=name,
    )(a, wt)


def _ba_kernel(heads, h_ref, w_ref, alog_ref, dtb_ref, o_ref):
    p = _dot_nt(h_ref[...], w_ref[...].astype(BF16))
    lane = lax.broadcasted_iota(jnp.int32, p.shape, 1)
    is_g = (lane >= heads) & (lane < 2 * heads)
    g = jnp.where(is_g, -jnp.exp(alog_ref[...]) * _softplus(p + dtb_ref[...]), 0.0)
    pos = lax.broadcasted_iota(jnp.int32, p.shape, 0) % CHUNK
    shift = 1
    while shift < CHUNK:
        g = g + jnp.where(pos >= shift, pltpu.roll(g, shift, axis=0), 0.0)
        shift *= 2
    o_ref[...] = jnp.where(is_g, g, _sigmoid(p))


def _ba(h, wt, layer, row0, a_log, dt_bias, heads):
    m, k = h.shape
    tm = min(GATE_ROWS, m)
    assert row0 % LANES == 0 and 2 * heads <= LANES
    pad = lambda v: jnp.zeros((1, LANES), F32).at[0, heads:2 * heads].set(v)
    return pl.pallas_call(
        functools.partial(_ba_kernel, heads),
        grid=(m // tm,),
        in_specs=[pl.BlockSpec((tm, k), lambda i: (i, 0)),
                  pl.BlockSpec((None, LANES, k), lambda i: (layer, row0 // LANES, 0)),
                  pl.BlockSpec((1, LANES), lambda i: (0, 0)),
                  pl.BlockSpec((1, LANES), lambda i: (0, 0))],
        out_specs=pl.BlockSpec((tm, LANES), lambda i: (i, 0)),
        out_shape=jax.ShapeDtypeStruct((m, LANES), F32),
        compiler_params=_params("parallel"),
        name="dn_gates",
    )(h, wt, pad(a_log), pad(dt_bias))


def _band(tm, lo, hi):
    ri = lax.broadcasted_iota(jnp.int32, (tm, tm + HALO), 0)
    ci = lax.broadcasted_iota(jnp.int32, (tm, tm + HALO), 1)
    back = ri + HALO - ci
    return jnp.where((back >= lo) & (back <= hi), 1.0, 0.0).astype(BF16)


def _pool_kernel(per_seq, u_ref, halo_ref, pw_ref, sc_ref, o_ref, ext_ref):
    tm, width = u_ref.shape
    group = width // len(POOL_WINDOWS)
    t = pl.program_id(0) % per_seq
    ext_ref[0:HALO, :] = jnp.where(t == 0, jnp.zeros(halo_ref.shape, halo_ref.dtype), halo_ref[...])
    ext_ref[HALO:, :] = u_ref[...]
    pos = t * tm + lax.broadcasted_iota(jnp.int32, (tm, 1), 0)
    for gi, win in enumerate(POOL_WINDOWS):
        cols = slice(gi * group, (gi + 1) * group)
        window_sum = _dot(_band(tm, 0, win - 1), ext_ref[:, cols])
        inv_count = 1.0 / jnp.minimum(pos + 1, win).astype(F32)
        pa = window_sum * inv_count - u_ref[:, cols].astype(F32)
        r = _dot(pa.astype(BF16), pw_ref[gi].astype(BF16)) * sc_ref[:, cols]
        o_ref[:, cols] = r.astype(o_ref.dtype)


def _pool(proj, pool_w, layer, pool_scale, seq):
    m = proj.shape[0]
    _, groups, group, _ = pool_w.shape
    width = groups * group
    tm = min(ELEMENTWISE_ROWS, seq)
    per_seq = seq // tm
    return pl.pallas_call(
        functools.partial(_pool_kernel, per_seq),
        grid=(m // tm,),
        in_specs=[pl.BlockSpec((tm, width), lambda i: (i, 0)),
                  pl.BlockSpec((HALO, width), lambda i: (jnp.maximum(i * (tm // HALO) - 1, 0), 0)),
                  pl.BlockSpec((None, groups, group, group), lambda i: (layer, 0, 0, 0)),
                  pl.BlockSpec((1, width), lambda i: (0, 0))],
        out_specs=pl.BlockSpec((tm, width), lambda i: (i, 0)),
        out_shape=jax.ShapeDtypeStruct((m, width), BF16),
        scratch_shapes=[pltpu.VMEM((tm + HALO, width), BF16)],
        compiler_params=_params("parallel"),
        name="pool_mixer",
    )(proj, proj, pool_w, pool_scale.reshape(1, width))


def _qkv_kernel(per_seq, q_ref, k_ref, v_ref, hq_ref, hk_ref, hv_ref, wq_ref, wk_ref, wv_ref,
                qo_ref, ko_ref, vo_ref, ext_ref):
    tm, width = q_ref.shape
    first = pl.program_id(0) % per_seq == 0
    taps = DN_CONV_TAPS
    shifts = jnp.concatenate([_band(tm, s, s) for s in range(1, taps)], axis=0)
    chunk = 2 * HEAD_DIM
    for x_ref, halo_ref, w_ref, o_ref, norm in (
            (q_ref, hq_ref, wq_ref, qo_ref, HEAD_DIM ** -0.5),
            (k_ref, hk_ref, wk_ref, ko_ref, 1.0),
            (v_ref, hv_ref, wv_ref, vo_ref, None)):
        ext_ref[0:HALO, :] = jnp.where(first, jnp.zeros(halo_ref.shape, halo_ref.dtype), halo_ref[...])
        ext_ref[HALO:, :] = x_ref[...]
        for c0 in range(0, width, chunk):
            wide = slice(c0, c0 + chunk)
            back = _dot(shifts, ext_ref[:, wide])
            y = x_ref[:, wide].astype(F32) * w_ref[taps - 1:taps, wide]
            for s in range(1, taps):
                y = y + back[(s - 1) * tm:s * tm] * w_ref[taps - 1 - s:taps - s, wide]
            y = _silu(y)
            for c in range(0, chunk, HEAD_DIM):
                yh = y[:, c:c + HEAD_DIM]
                if norm is not None:
                    yh = yh * (lax.rsqrt(jnp.sum(yh * yh, axis=-1, keepdims=True) + EPS) * norm)
                o_ref[:, c0 + c:c0 + c + HEAD_DIM] = yh.astype(o_ref.dtype)


def _qkv_conv(proj, dn_conv, width, col0, seq):
    m = proj.shape[0]
    tm = min(CONV_ROWS, seq)
    per_seq = seq // tm
    b0 = col0 // width
    rows = [pl.BlockSpec((tm, width), functools.partial(lambda s, i: (i, b0 + s), s)) for s in range(3)]
    halos = [pl.BlockSpec((HALO, width),
                          functools.partial(lambda s, i: (jnp.maximum(i * (tm // HALO) - 1, 0), b0 + s), s))
             for s in range(3)]
    taps = [pl.BlockSpec((DN_CONV_TAPS, width), functools.partial(lambda s, i: (0, s), s)) for s in range(3)]
    out = pl.BlockSpec((tm, width), lambda i: (i, 0))
    shp = jax.ShapeDtypeStruct((m, width), BF16)
    return pl.pallas_call(
        functools.partial(_qkv_kernel, per_seq),
        grid=(m // tm,),
        in_specs=rows + halos + taps,
        out_specs=[out, out, out],
        out_shape=[shp, shp, shp],
        scratch_shapes=[pltpu.VMEM((tm + HALO, width), BF16)],
        compiler_params=_params("parallel"),
        name="dn_qkv_conv",
    )(proj, proj, proj, proj, proj, proj, dn_conv, dn_conv, dn_conv)


def _unit_lower_inverses(lows):
    n = lows[0].shape[0]
    ri = lax.broadcasted_iota(jnp.int32, (n, n), 0)
    ci = lax.broadcasted_iota(jnp.int32, (n, n), 1)
    same = lambda bits: jnp.right_shift(ri, bits) == jnp.right_shift(ci, bits)
    eye = jnp.where(ri == ci, 1.0, 0.0)
    invs = [eye - jnp.where(same(1), low, 0.0) for low in lows]
    bits = 1
    while (1 << bits) < n:
        mask = same(bits + 1) & jnp.logical_not(same(bits))
        inv16 = [inv.astype(BF16) for inv in invs]
        right = [_dot(jnp.where(mask, low, 0.0).astype(BF16), i16) for low, i16 in zip(lows, inv16)]
        invs = [inv - _dot(i16, r.astype(BF16)) for inv, i16, r in zip(invs, inv16, right)]
        bits += 1
    return invs


def _ride_along_blocks(rows, steps):
    block = next(b for b in range(16, rows + 1, 16) if rows % b == 0 and rows // b <= steps)
    return block, rows // block


def _delta_kernel(heads, q_ref, k_ref, v_ref, z_ref, g_ref, gt_ref, nw_ref, wsrc_ref, o_ref, w16_ref,
                  state_ref):
    rows, width = q_ref.shape
    hp = width // HEAD_DIM
    c = CHUNK
    w16_ref[...] = wsrc_ref[...].astype(BF16)

    @pl.when(pl.program_id(2) == 0)
    def _():
        state_ref[...] = jnp.zeros_like(state_ref)

    gates = g_ref[...]
    lane = lax.broadcasted_iota(jnp.int32, gates.shape, 1)
    ri = lax.broadcasted_iota(jnp.int32, (c, c), 0)
    ci = lax.broadcasted_iota(jnp.int32, (c, c), 1)
    causal = ri >= ci
    strict = ri > ci

    work = []
    for hh in range(hp):
        head = pl.program_id(1) * hp + hh
        cols = slice(hh * HEAD_DIM, (hh + 1) * HEAD_DIM)
        beta_col = jnp.sum(jnp.where(lane == head, gates, 0.0), axis=-1, keepdims=True)
        gc_col = jnp.sum(jnp.where(lane == heads + head, gates, 0.0), axis=-1, keepdims=True)
        gc_row = gt_ref[pl.ds(heads + head, 1), :]
        for r0 in range(0, rows, c):
            rsl = slice(r0, r0 + c)
            q = q_ref[rsl, cols]
            k16 = k_ref[rsl, cols]
            k = k16.astype(F32)
            beta = beta_col[rsl]
            gcc = gc_col[rsl]
            gcr = gc_row[:, rsl]
            g_last = gcr[:, c - 1:c]
            decay = jnp.exp(jnp.where(causal, gcc - gcr, -jnp.inf))
            e_g = jnp.exp(gcc)
            kb = k * beta
            kq_k = _dot_nt(jnp.concatenate([kb.astype(BF16), q], axis=0), k16)
            low = jnp.where(strict, kq_k[:c] * decay, 0.0)
            qk = jnp.where(causal, kq_k[c:] * decay, 0.0)
            work.append(dict(
                hh=hh, rsl=rsl, cols=cols, low=low,
                vb_kbg16=jnp.concatenate([(v_ref[rsl, cols].astype(F32) * beta).astype(BF16),
                                          (kb * e_g).astype(BF16)], axis=1),
                qd16=(q.astype(F32) * e_g).astype(BF16),
                qk_kdt16=jnp.concatenate([qk.astype(BF16),
                                          (k * jnp.exp(g_last - gcc)).T.astype(BF16)], axis=0),
                g_end=jnp.exp(g_last)))
    for item, t_inv in zip(work, _unit_lower_inverses([item["low"] for item in work])):
        uw = _dot(t_inv.astype(BF16), item["vb_kbg16"])
        item["u"] = uw[:, :HEAD_DIM]
        item["w_qd16"] = jnp.concatenate([uw[:, HEAD_DIM:].astype(BF16), item["qd16"]], axis=0)

    for hh in range(hp):
        state = state_ref[hh]
        for item in work:
            if item["hh"] != hh:
                continue
            from_state = _dot(item["w_qd16"], state.astype(BF16))
            vn16 = (item["u"] - from_state[:c]).astype(BF16)
            from_new = _dot(item["qk_kdt16"], vn16)
            item["o"] = from_state[c:] + from_new[:c]
            state = state * item["g_end"] + from_new[c:]
        state_ref[hh] = state

    for item in work:
        rsl, cols = item["rsl"], item["cols"]
        z = z_ref[rsl, cols].astype(F32)
        o_ref[rsl, cols] = (_rms(item["o"]) * nw_ref[...] * _silu(z)).astype(o_ref.dtype)


def _delta(q, k, v, proj, z_col0, gates, gates_t, dn_norm, batch, seq, heads, w_src, layer):
    m, width = q.shape
    hp = min(DELTA_HEADS_PER_STEP, heads)
    rows = min(CHUNK, seq)
    per_seq = seq // rows
    groups = heads // hp
    bw = hp * HEAD_DIM
    zb0 = z_col0 // bw
    _, w_rows, w_cols = w_src.shape
    w_block, w_blocks = _ride_along_blocks(w_rows, batch * groups * per_seq)
    w_idx = lambda b, h, t: jnp.minimum((b * groups + h) * per_seq + t, w_blocks - 1)
    qkv = pl.BlockSpec((rows, bw), lambda b, h, t: (b * per_seq + t, h))
    return pl.pallas_call(
        functools.partial(_delta_kernel, heads),
        grid=(batch, groups, per_seq),
        in_specs=[qkv, qkv, qkv,
                  pl.BlockSpec((rows, bw), lambda b, h, t: (b * per_seq + t, zb0 + h)),
                  pl.BlockSpec((rows, LANES), lambda b, h, t: (b * per_seq + t, 0)),
                  pl.BlockSpec((2 * heads, rows), lambda b, h, t: (0, b * per_seq + t)),
                  pl.BlockSpec((1, HEAD_DIM), lambda b, h, t: (0, 0)),
                  pl.BlockSpec((None, w_block, w_cols), lambda b, h, t: (layer, w_idx(b, h, t), 0))],
        out_specs=[qkv, pl.BlockSpec((w_block, w_cols), lambda b, h, t: (w_idx(b, h, t), 0))],
        out_shape=[jax.ShapeDtypeStruct((m, width), BF16), jax.ShapeDtypeStruct((w_rows, w_cols), BF16)],
        scratch_shapes=[pltpu.VMEM((hp, HEAD_DIM, HEAD_DIM), F32)],
        compiler_params=_params("arbitrary", "arbitrary", "arbitrary"),
        name="gated_delta",
    )(q, k, v, proj, gates, gates_t, dn_norm.reshape(1, HEAD_DIM), w_src)


def _merge_kernel(ga_ref, gb_ref, pa_ref, ob_ref, pp_ref, dp_ref, y_ref):
    ya = _sigmoid(ga_ref[...].astype(F32)) * _dot(pa_ref[...], pp_ref[...].astype(BF16))
    yb = _sigmoid(gb_ref[...].astype(F32)) * _dot(ob_ref[...], dp_ref[...].astype(BF16))
    y_ref[...] = (ya + yb).astype(y_ref.dtype)


def _merge(proj, gate_col0, pa, ob, pool_proj, dn_proj, layer):
    m, kp = pa.shape
    d = pool_proj.shape[2]
    tm, tn = min(MATMUL_ROWS, m), min(MATMUL_COLS, d)
    nb = d // tn
    assert gate_col0 % tn == 0
    g0 = gate_col0 // tn
    return pl.pallas_call(
        _merge_kernel,
        grid=(nb, m // tm),
        in_specs=[pl.BlockSpec((tm, tn), lambda j, i: (i, g0 + j)),
                  pl.BlockSpec((tm, tn), lambda j, i: (i, g0 + nb + j)),
                  pl.BlockSpec((tm, kp), lambda j, i: (i, 0)),
                  pl.BlockSpec((tm, kp), lambda j, i: (i, 0)),
                  pl.BlockSpec((None, kp, tn), lambda j, i: (layer, 0, j)),
                  pl.BlockSpec((None, kp, tn), lambda j, i: (layer, 0, j))],
        out_specs=pl.BlockSpec((tm, tn), lambda j, i: (i, j)),
        out_shape=jax.ShapeDtypeStruct((m, d), BF16),
        compiler_params=_params("parallel", "arbitrary"),
        name="gated_merge",
    )(proj, proj, pa, ob, pool_proj, dn_proj)


FFN_CARRY = 8


def _ffn_up_kernel(per_seq, ragged, h_ref, wa_ref, wu_ref, cw_ref, cb_ref, wsrc_ref, o_ref, w16_ref, a_ref):
    tm, tn = o_ref.shape
    half = tn // 2
    w16_ref[...] = wsrc_ref[...].astype(BF16)

    @pl.when(pl.program_id(1) % per_seq == 0)
    def _():
        a_ref[0:FFN_CARRY, :] = jnp.zeros((FFN_CARRY, tn), F32)

    def tile(width):
        cols = slice(0, width)
        h = h_ref[...]
        a_ref[FFN_CARRY:, cols] = _dot(h, wa_ref[:, cols])
        u = _dot(h, wu_ref[:, tn - width:])
        a = cb_ref[:, cols]
        for tap in range(FFN_CONV_TAPS):
            r0 = FFN_CARRY - (FFN_CONV_TAPS - 1) + tap
            a = a + a_ref[r0:r0 + tm, cols] * cw_ref[tap:tap + 1, cols]
        o_ref[:, cols] = (_silu(a) * u).astype(o_ref.dtype)
        a_ref[0:FFN_CARRY, cols] = a_ref[tm:tm + FFN_CARRY, cols]

    if ragged:
        last = pl.program_id(0) == pl.num_programs(0) - 1
        pl.when(jnp.logical_not(last))(lambda: tile(tn))
        pl.when(last)(lambda: tile(half))
    else:
        tile(tn)


def _ffn_up(h, w_up, layer, conv_w, conv_b, seq, w_src):
    m, d = h.shape
    n = w_up.shape[1] // 2
    tm, tn = min(MATMUL_ROWS, seq), MATMUL_COLS
    half = tn // 2
    assert n % half == 0
    nh = n // half
    nt = pl.cdiv(n, tn)
    per_seq = seq // tm
    pad = nt * tn - n
    conv_w = jnp.pad(conv_w, ((0, 0), (0, 0), (0, pad)))
    conv_b = jnp.pad(conv_b, ((0, 0), (0, pad))).reshape(conv_b.shape[0], 1, nt * tn)
    row_tiles = m // tm
    _, w_rows, w_cols = w_src.shape
    w_block, w_blocks = _ride_along_blocks(w_rows, nt * row_tiles)
    w_idx = lambda j, i: jnp.minimum(j * row_tiles + i, w_blocks - 1)
    return pl.pallas_call(
        functools.partial(_ffn_up_kernel, per_seq, n % tn != 0),
        grid=(nt, row_tiles),
        in_specs=[pl.BlockSpec((tm, d), lambda j, i: (i, 0)),
                  pl.BlockSpec((d, tn), lambda j, i: (0, j)),
                  pl.BlockSpec((pl.Element(d), pl.Element(tn)),
                               lambda j, i: (0, pl.multiple_of(jnp.minimum(n + j * tn, 2 * n - tn), LANES))),
                  pl.BlockSpec((None, FFN_CONV_TAPS, tn), lambda j, i: (layer, 0, j)),
                  pl.BlockSpec((None, 1, tn), lambda j, i: (layer, 0, j)),
                  pl.BlockSpec((None, w_block, w_cols), lambda j, i: (layer, w_idx(j, i), 0))],
        out_specs=[pl.BlockSpec((tm, tn), lambda j, i: (i, j)),
                   pl.BlockSpec((w_block, w_cols), lambda j, i: (w_idx(j, i), 0))],
        out_shape=[jax.ShapeDtypeStruct((m, n), BF16), jax.ShapeDtypeStruct((w_rows, w_cols), BF16)],
        scratch_shapes=[pltpu.VMEM((tm + FFN_CARRY, tn), F32)],
        compiler_params=_params("arbitrary", "arbitrary"),
        name="ffn_up_conv_gate",
    )(h, w_up, w_up, conv_w, conv_b, w_src)


def kernel(x, c, w_ada, b_ada, ada_table, mix_pre_norm, w_in, pool_w, pool_scale, pool_proj,
           dn_conv, dn_a_log, dn_dt_bias, dn_norm, dn_proj, w_o, mix_post_norm,
           ffn_pre_norm, w_up, ffn_conv, ffn_conv_b, w_down, ffn_post_norm):
    batch, seq, d = x.shape
    depth = w_in.shape[0]
    m = batch * seq
    pool_width = pool_scale.shape[1]
    dn_width = dn_proj.shape[1]
    heads = dn_a_log.shape[1]
    c_qkv = pool_width
    c_z = c_qkv + 3 * dn_width
    c_ba = c_z + dn_width

    c8 = jnp.pad(c, ((0, SUBLANES - batch), (0, 0)))
    mod = _ada(c8, w_ada, b_ada, ada_table.reshape(depth, N_MOD * d))
    mod = mod[:, :batch].reshape(depth, batch, N_MOD, 1, d)

    xf = x.reshape(m, d)
    h = _norm_mod(xf, mix_pre_norm[0], mod[0, :, 1], mod[0, :, 0], seq)
    w_in_t = jnp.swapaxes(w_in, 1, 2)
    for l in range(depth):
        shift_f, scale_f, gate_m, gate_f = mod[l, :, 3], mod[l, :, 4], mod[l, :, 2], mod[l, :, 5]
        proj = _matmul_nt(h, w_in_t, l, c_ba + 2 * d, c_ba, 2 * heads, BF16, IN_PROJ_ROWS, MATMUL_COLS, "in_proj")
        gates = _ba(h, w_in_t, l, c_ba, dn_a_log[l], dn_dt_bias[l], heads)
        gates_t = gates[:, :2 * heads].T
        pa = _pool(proj, pool_w, l, pool_scale[l], seq)
        q, k, v = _qkv_conv(proj, dn_conv[l], dn_width, c_qkv, seq)
        ob, w_up16 = _delta(q, k, v, proj, c_z, gates, gates_t, dn_norm[l], batch, seq, heads, w_up, l)
        y = _merge(proj, c_ba, pa, ob, pool_proj, dn_proj, l)
        yo = _matmul(y, w_o, l, d, BF16, IN_PROJ_ROWS, MATMUL_COLS, "out_proj", rows_outer=True)
        xf, h = _resid_norm(yo, xf, mix_post_norm[l], gate_m, seq,
                            nxt=(ffn_pre_norm[l], scale_f, shift_f))
        act, w_down16 = _ffn_up(h, w_up16, l, ffn_conv, ffn_conv_b, seq, w_down)
        yd = _matmul(act, w_down16[None], 0, d, BF16, FFN_DOWN_ROWS, MATMUL_COLS, "ffn_down")
        if l + 1 < depth:
            xf, h = _resid_norm(yd, xf, ffn_post_norm[l], gate_f, seq,
                                nxt=(mix_pre_norm[l + 1], mod[l + 1, :, 1], mod[l + 1, :, 0]))
        else:
            xf = _resid_norm(yd, xf, ffn_post_norm[l], gate_f, seq)
    return xf.reshape(batch, seq, d)
```

```python
import functools

import jax
import jax.numpy as jnp
from jax import lax
from jax.experimental import pallas as pl
from jax.experimental.pallas import tpu as pltpu

F32 = jnp.float32
BF16 = jnp.bfloat16

EPS = 1e-6
POOL_WINDOWS = (2, 4, 8, 16)
HEAD_DIM = 128
DN_CONV_TAPS = 4
FFN_CONV_TAPS = 3
N_MOD = 6
CHUNK = 128
HALO = 16
VMEM_LIMIT_BYTES = 60 * 1024 * 1024
LANES = 128
SUBLANES = 8

ELEMENTWISE_ROWS = 512
CONV_ROWS = 128
GATE_ROWS = 2048
MATMUL_ROWS, MATMUL_COLS = 1024, 512
IN_PROJ_ROWS = 2048
FFN_DOWN_ROWS = 512
DELTA_HEADS_PER_STEP = 16
ADA_COLS = 1024


def _params(*semantics):
    return pltpu.CompilerParams(dimension_semantics=semantics,
                                vmem_limit_bytes=VMEM_LIMIT_BYTES)


def _dot(a, b):
    return jnp.dot(a, b, preferred_element_type=F32)


def _dot_nt(a, b):
    return lax.dot_general(a, b, (((1,), (1,)), ((), ())), preferred_element_type=F32)


def _sigmoid(x):
    return 0.5 + 0.5 * jnp.tanh(0.5 * x)


def _silu(x):
    return x * _sigmoid(x)


def _softplus(x):
    return jnp.maximum(x, 0.0) + jnp.log1p(jnp.exp(-jnp.abs(x)))


def _rms(x):
    return x * lax.rsqrt(jnp.mean(x * x, axis=-1, keepdims=True) + EPS)


def _ada_kernel(c_ref, w_ref, b_ref, t_ref, o_ref):
    a = _silu(c_ref[...]).astype(BF16)
    acc = _dot(a, w_ref[...].astype(BF16)) + b_ref[...]
    o_ref[...] = acc[None] + t_ref[...]


def _ada(c8, w_ada, b_ada, table):
    depth = table.shape[0]
    d, n = w_ada.shape
    tn = min(ADA_COLS, n)
    return pl.pallas_call(
        _ada_kernel,
        grid=(n // tn,),
        in_specs=[pl.BlockSpec((SUBLANES, d), lambda j: (0, 0)),
                  pl.BlockSpec((d, tn), lambda j: (0, j)),
                  pl.BlockSpec((1, tn), lambda j: (0, j)),
                  pl.BlockSpec((depth, 1, tn), lambda j: (0, 0, j))],
        out_specs=pl.BlockSpec((depth, SUBLANES, tn), lambda j: (0, 0, j)),
        out_shape=jax.ShapeDtypeStruct((depth, SUBLANES, n), F32),
        compiler_params=_params("arbitrary"),
        name="ada",
    )(c8, w_ada, b_ada.reshape(1, n), table.reshape(depth, 1, n))


def _norm_mod_kernel(x_ref, w_ref, sc_ref, sh_ref, h_ref):
    y = _rms(x_ref[...]) * w_ref[...]
    h_ref[...] = (y * (1.0 + sc_ref[0]) + sh_ref[0]).astype(h_ref.dtype)


def _norm_mod(x, w, scale, shift, seq):
    m, d = x.shape
    tm = min(ELEMENTWISE_ROWS, seq)
    per_seq = seq // tm
    row = pl.BlockSpec((tm, d), lambda i: (i, 0))
    vec = pl.BlockSpec((1, d), lambda i: (0, 0))
    bvec = pl.BlockSpec((1, 1, d), lambda i: (i // per_seq, 0, 0))
    return pl.pallas_call(
        _norm_mod_kernel,
        grid=(m // tm,),
        in_specs=[row, vec, bvec, bvec],
        out_specs=row,
        out_shape=jax.ShapeDtypeStruct((m, d), BF16),
        compiler_params=_params("parallel"),
        name="norm_mod",
    )(x, w.reshape(1, d), scale, shift)


def _resid_norm_kernel(y_ref, x_ref, pw_ref, g_ref, nw_ref, sc_ref, sh_ref, xo_ref, h_ref):
    xn = x_ref[...] + g_ref[0] * (_rms(y_ref[...].astype(F32)) * pw_ref[...])
    xo_ref[...] = xn
    h = _rms(xn) * nw_ref[...]
    h_ref[...] = (h * (1.0 + sc_ref[0]) + sh_ref[0]).astype(h_ref.dtype)


def _resid_kernel(y_ref, x_ref, pw_ref, g_ref, xo_ref):
    xo_ref[...] = x_ref[...] + g_ref[0] * (_rms(y_ref[...].astype(F32)) * pw_ref[...])


def _resid_norm(y, x, post_w, gate, seq, nxt=None):
    m, d = x.shape
    tm = min(ELEMENTWISE_ROWS, seq)
    per_seq = seq // tm
    row = pl.BlockSpec((tm, d), lambda i: (i, 0))
    vec = pl.BlockSpec((1, d), lambda i: (0, 0))
    bvec = pl.BlockSpec((1, 1, d), lambda i: (i // per_seq, 0, 0))
    if nxt is None:
        return pl.pallas_call(
            _resid_kernel,
            grid=(m // tm,),
            in_specs=[row, row, vec, bvec],
            out_specs=row,
            out_shape=jax.ShapeDtypeStruct((m, d), F32),
            compiler_params=_params("parallel"),
            name="resid",
        )(y, x, post_w.reshape(1, d), gate)
    norm_w, scale, shift = nxt
    return pl.pallas_call(
        _resid_norm_kernel,
        grid=(m // tm,),
        in_specs=[row, row, vec, bvec, vec, bvec, bvec],
        out_specs=[row, row],
        out_shape=[jax.ShapeDtypeStruct((m, d), F32), jax.ShapeDtypeStruct((m, d), BF16)],
        compiler_params=_params("parallel"),
        name="resid_norm",
    )(y, x, post_w.reshape(1, d), gate, norm_w.reshape(1, d), scale, shift)


def _mm_kernel(a_ref, w_ref, o_ref):
    o_ref[...] = _dot(a_ref[...], w_ref[...].astype(BF16)).astype(o_ref.dtype)


def _matmul(a, w, layer, n, out_dtype, tm, tn, name, rows_outer=False):
    m, k = a.shape
    tm, tn = min(tm, m), min(tn, n)
    if rows_outer:
        grid, ij = (m // tm, n // tn), lambda i, j: (i, j)
    else:
        grid, ij = (n // tn, m // tm), lambda j, i: (i, j)
    return pl.pallas_call(
        _mm_kernel,
        grid=grid,
        in_specs=[pl.BlockSpec((tm, k), lambda *g: (ij(*g)[0], 0)),
                  pl.BlockSpec((None, k, tn), lambda *g: (layer, 0, ij(*g)[1]))],
        out_specs=pl.BlockSpec((tm, tn), lambda *g: ij(*g)),
        out_shape=jax.ShapeDtypeStruct((m, n), out_dtype),
        compiler_params=_params("parallel", "arbitrary"),
        name=name,
    )(a, w)


def _mm_nt_kernel(a_ref, wt_ref, o_ref):
    o_ref[...] = _dot_nt(a_ref[...], wt_ref[0].astype(BF16)).astype(o_ref.dtype)


def _matmul_nt(a, wt, layer, n, skip_at, skip, out_dtype, tm, tn, name):
    m, k = a.shape
    tm, tn = min(tm, m), min(tn, n)
    sublanes = 8 * 4 // wt.dtype.itemsize
    assert skip % sublanes == 0 and n % tn == 0 and skip_at % tn == 0
    first_after = skip_at // tn
    row = lambda j: pl.multiple_of(j * tn + jnp.where(j >= first_after, skip, 0), sublanes)
    return pl.pallas_call(
        _mm_nt_kernel,
        grid=(m // tm, n // tn),
        in_specs=[pl.BlockSpec((tm, k), lambda i, j: (i, 0)),
                  pl.BlockSpec((pl.Element(1), pl.Element(tn), pl.Element(k)),
                               lambda i, j: (layer, row(j), 0))],
        out_specs=pl.BlockSpec((tm, tn), lambda i, j: (i, j)),
        out_shape=jax.ShapeDtypeStruct((m, n), out_dtype),
        compiler_params=_params("parallel", "arbitrary"),
        name=name,
    )(a, wt)


def _ba_kernel(heads, h_ref, w_ref, alog_ref, dtb_ref, o_ref):
    p = _dot_nt(h_ref[...], w_ref[...].astype(BF16))
    lane = lax.broadcasted_iota(jnp.int32, p.shape, 1)
    is_g = (lane >= heads) & (lane < 2 * heads)
    g = jnp.where(is_g, -jnp.exp(alog_ref[...]) * _softplus(p + dtb_ref[...]), 0.0)
    pos = lax.broadcasted_iota(jnp.int32, p.shape, 0) % CHUNK
    shift = 1
    while shift < CHUNK:
        g = g + jnp.where(pos >= shift, pltpu.roll(g, shift, axis=0), 0.0)
        shift *= 2
    o_ref[...] = jnp.where(is_g, g, _sigmoid(p))


def _ba(h, wt, layer, row0, a_log, dt_bias, heads):
    m, k = h.shape
    tm = min(GATE_ROWS, m)
    assert row0 % LANES == 0 and 2 * heads <= LANES
    pad = lambda v: jnp.zeros((1, LANES), F32).at[0, heads:2 * heads].set(v)
    return pl.pallas_call(
        functools.partial(_ba_kernel, heads),
        grid=(m // tm,),
        in_specs=[pl.BlockSpec((tm, k), lambda i: (i, 0)),
                  pl.BlockSpec((None, LANES, k), lambda i: (layer, row0 // LANES, 0)),
                  pl.BlockSpec((1, LANES), lambda i: (0, 0)),
                  pl.BlockSpec((1, LANES), lambda i: (0, 0))],
        out_specs=pl.BlockSpec((tm, LANES), lambda i: (i, 0)),
        out_shape=jax.ShapeDtypeStruct((m, LANES), F32),
        compiler_params=_params("parallel"),
        name="dn_gates",
    )(h, wt, pad(a_log), pad(dt_bias))


def _band(tm, lo, hi):
    ri = lax.broadcasted_iota(jnp.int32, (tm, tm + HALO), 0)
    ci = lax.broadcasted_iota(jnp.int32, (tm, tm + HALO), 1)
    back = ri + HALO - ci
    return jnp.where((back >= lo) & (back <= hi), 1.0, 0.0).astype(BF16)


def _pool_kernel(per_seq, u_ref, halo_ref, pw_ref, sc_ref, o_ref, ext_ref):
    tm, width = u_ref.shape
    group = width // len(POOL_WINDOWS)
    t = pl.program_id(0) % per_seq
    ext_ref[0:HALO, :] = jnp.where(t == 0, jnp.zeros(halo_ref.shape, halo_ref.dtype), halo_ref[...])
    ext_ref[HALO:, :] = u_ref[...]
    pos = t * tm + lax.broadcasted_iota(jnp.int32, (tm, 1), 0)
    for gi, win in enumerate(POOL_WINDOWS):
        cols = slice(gi * group, (gi + 1) * group)
        window_sum = _dot(_band(tm, 0, win - 1), ext_ref[:, cols])
        inv_count = 1.0 / jnp.minimum(pos + 1, win).astype(F32)
        pa = window_sum * inv_count - u_ref[:, cols].astype(F32)
        r = _dot(pa.astype(BF16), pw_ref[gi].astype(BF16)) * sc_ref[:, cols]
        o_ref[:, cols] = r.astype(o_ref.dtype)


def _pool(proj, pool_w, layer, pool_scale, seq):
    m = proj.shape[0]
    _, groups, group, _ = pool_w.shape
    width = groups * group
    tm = min(ELEMENTWISE_ROWS, seq)
    per_seq = seq // tm
    return pl.pallas_call(
        functools.partial(_pool_kernel, per_seq),
        grid=(m // tm,),
        in_specs=[pl.BlockSpec((tm, width), lambda i: (i, 0)),
                  pl.BlockSpec((HALO, width), lambda i: (jnp.maximum(i * (tm // HALO) - 1, 0), 0)),
                  pl.BlockSpec((None, groups, group, group), lambda i: (layer, 0, 0, 0)),
                  pl.BlockSpec((1, width), lambda i: (0, 0))],
        out_specs=pl.BlockSpec((tm, width), lambda i: (i, 0)),
        out_shape=jax.ShapeDtypeStruct((m, width), BF16),
        scratch_shapes=[pltpu.VMEM((tm + HALO, width), BF16)],
        compiler_params=_params("parallel"),
        name="pool_mixer",
    )(proj, proj, pool_w, pool_scale.reshape(1, width))


def _qkv_kernel(per_seq, q_ref, k_ref, v_ref, hq_ref, hk_ref, hv_ref, wq_ref, wk_ref, wv_ref,
                qo_ref, ko_ref, vo_ref, ext_ref):
    tm, width = q_ref.shape
    first = pl.program_id(0) % per_seq == 0
    taps = DN_CONV_TAPS
    shifts = jnp.concatenate([_band(tm, s, s) for s in range(1, taps)], axis=0)
    chunk = 2 * HEAD_DIM
    for x_ref, halo_ref, w_ref, o_ref, norm in (
            (q_ref, hq_ref, wq_ref, qo_ref, HEAD_DIM ** -0.5),
            (k_ref, hk_ref, wk_ref, ko_ref, 1.0),
            (v_ref, hv_ref, wv_ref, vo_ref, None)):
        ext_ref[0:HALO, :] = jnp.where(first, jnp.zeros(halo_ref.shape, halo_ref.dtype), halo_ref[...])
        ext_ref[HALO:, :] = x_ref[...]
        for c0 in range(0, width, chunk):
            wide = slice(c0, c0 + chunk)
            back = _dot(shifts, ext_ref[:, wide])
            y = x_ref[:, wide].astype(F32) * w_ref[taps - 1:taps, wide]
            for s in range(1, taps):
                y = y + back[(s - 1) * tm:s * tm] * w_ref[taps - 1 - s:taps - s, wide]
            y = _silu(y)
            for c in range(0, chunk, HEAD_DIM):
                yh = y[:, c:c + HEAD_DIM]
                if norm is not None:
                    yh = yh * (lax.rsqrt(jnp.sum(yh * yh, axis=-1, keepdims=True) + EPS) * norm)
                o_ref[:, c0 + c:c0 + c + HEAD_DIM] = yh.astype(o_ref.dtype)


def _qkv_conv(proj, dn_conv, width, col0, seq):
    m = proj.shape[0]
    tm = min(CONV_ROWS, seq)
    per_seq = seq // tm
    b0 = col0 // width
    rows = [pl.BlockSpec((tm, width), functools.partial(lambda s, i: (i, b0 + s), s)) for s in range(3)]
    halos = [pl.BlockSpec((HALO, width),
                          functools.partial(lambda s, i: (jnp.maximum(i * (tm // HALO) - 1, 0), b0 + s), s))
             for s in range(3)]
    taps = [pl.BlockSpec((DN_CONV_TAPS, width), functools.partial(lambda s, i: (0, s), s)) for s in range(3)]
    out = pl.BlockSpec((tm, width), lambda i: (i, 0))
    shp = jax.ShapeDtypeStruct((m, width), BF16)
    return pl.pallas_call(
        functools.partial(_qkv_kernel, per_seq),
        grid=(m // tm,),
        in_specs=rows + halos + taps,
        out_specs=[out, out, out],
        out_shape=[shp, shp, shp],
        scratch_shapes=[pltpu.VMEM((tm + HALO, width), BF16)],
        compiler_params=_params("parallel"),
        name="dn_qkv_conv",
    )(proj, proj, proj, proj, proj, proj, dn_conv, dn_conv, dn_conv)


def _unit_lower_inverses(lows):
    n = lows[0].shape[0]
    ri = lax.broadcasted_iota(jnp.int32, (n, n), 0)
    ci = lax.broadcasted_iota(jnp.int32, (n, n), 1)
    same = lambda bits: jnp.right_shift(ri, bits) == jnp.right_shift(ci, bits)
    eye = jnp.where(ri == ci, 1.0, 0.0)
    invs = [eye - jnp.where(same(1), low, 0.0) for low in lows]
    bits = 1
    while (1 << bits) < n:
        mask = same(bits + 1) & jnp.logical_not(same(bits))
        inv16 = [inv.astype(BF16) for inv in invs]
        right = [_dot(jnp.where(mask, low, 0.0).astype(BF16), i16) for low, i16 in zip(lows, inv16)]
        invs = [inv - _dot(i16, r.astype(BF16)) for inv, i16, r in zip(invs, inv16, right)]
        bits += 1
    return invs


def _ride_along_blocks(rows, steps):
    block = next(b for b in range(16, rows + 1, 16) if rows % b == 0 and rows // b <= steps)
    return block, rows // block


def _delta_kernel(heads, q_ref, k_ref, v_ref, z_ref, g_ref, gt_ref, nw_ref, wsrc_ref, o_ref, w16_ref,
                  state_ref):
    rows, width = q_ref.shape
    hp = width // HEAD_DIM
    c = CHUNK
    w16_ref[...] = wsrc_ref[...].astype(BF16)

    @pl.when(pl.program_id(2) == 0)
    def _():
        state_ref[...] = jnp.zeros_like(state_ref)

    gates = g_ref[...]
    lane = lax.broadcasted_iota(jnp.int32, gates.shape, 1)
    ri = lax.broadcasted_iota(jnp.int32, (c, c), 0)
    ci = lax.broadcasted_iota(jnp.int32, (c, c), 1)
    causal = ri >= ci
    strict = ri > ci

    work = []
    for hh in range(hp):
        head = pl.program_id(1) * hp + hh
        cols = slice(hh * HEAD_DIM, (hh + 1) * HEAD_DIM)
        beta_col = jnp.sum(jnp.where(lane == head, gates, 0.0), axis=-1, keepdims=True)
        gc_col = jnp.sum(jnp.where(lane == heads + head, gates, 0.0), axis=-1, keepdims=True)
        gc_row = gt_ref[pl.ds(heads + head, 1), :]
        for r0 in range(0, rows, c):
            rsl = slice(r0, r0 + c)
            q = q_ref[rsl, cols]
            k16 = k_ref[rsl, cols]
            k = k16.astype(F32)
            beta = beta_col[rsl]
            gcc = gc_col[rsl]
            gcr = gc_row[:, rsl]
            g_last = gcr[:, c - 1:c]
            decay = jnp.exp(jnp.where(causal, gcc - gcr, -jnp.inf))
            e_g = jnp.exp(gcc)
            kb = k * beta
            kq_k = _dot_nt(jnp.concatenate([kb.astype(BF16), q], axis=0), k16)
            low = jnp.where(strict, kq_k[:c] * decay, 0.0)
            qk = jnp.where(causal, kq_k[c:] * decay, 0.0)
            work.append(dict(
                hh=hh, rsl=rsl, cols=cols, low=low,
                vb_kbg16=jnp.concatenate([(v_ref[rsl, cols].astype(F32) * beta).astype(BF16),
                                          (kb * e_g).astype(BF16)], axis=1),
                qd16=(q.astype(F32) * e_g).astype(BF16),
                qk_kdt16=jnp.concatenate([qk.astype(BF16),
                                          (k * jnp.exp(g_last - gcc)).T.astype(BF16)], axis=0),
                g_end=jnp.exp(g_last)))
    for item, t_inv in zip(work, _unit_lower_inverses([item["low"] for item in work])):
        uw = _dot(t_inv.astype(BF16), item["vb_kbg16"])
        item["u"] = uw[:, :HEAD_DIM]
        item["w_qd16"] = jnp.concatenate([uw[:, HEAD_DIM:].astype(BF16), item["qd16"]], axis=0)

    for hh in range(hp):
        state = state_ref[hh]
        for item in work:
            if item["hh"] != hh:
                continue
            from_state = _dot(item["w_qd16"], state.astype(BF16))
            vn16 = (item["u"] - from_state[:c]).astype(BF16)
            from_new = _dot(item["qk_kdt16"], vn16)
            item["o"] = from_state[c:] + from_new[:c]
            state = state * item["g_end"] + from_new[c:]
        state_ref[hh] = state

    for item in work:
        rsl, cols = item["rsl"], item["cols"]
        z = z_ref[rsl, cols].astype(F32)
        o_ref[rsl, cols] = (_rms(item["o"]) * nw_ref[...] * _silu(z)).astype(o_ref.dtype)


def _delta(q, k, v, proj, z_col0, gates, gates_t, dn_norm, batch, seq, heads, w_src, layer):
    m, width = q.shape
    hp = min(DELTA_HEADS_PER_STEP, heads)
    rows = min(CHUNK, seq)
    per_seq = seq // rows
    groups = heads // hp
    bw = hp * HEAD_DIM
    zb0 = z_col0 // bw
    _, w_rows, w_cols = w_src.shape
    w_block, w_blocks = _ride_along_blocks(w_rows, batch * groups * per_seq)
    w_idx = lambda b, h, t: jnp.minimum((b * groups + h) * per_seq + t, w_blocks - 1)
    qkv = pl.BlockSpec((rows, bw), lambda b, h, t: (b * per_seq + t, h))
    return pl.pallas_call(
        functools.partial(_delta_kernel, heads),
        grid=(batch, groups, per_seq),
        in_specs=[qkv, qkv, qkv,
                  pl.BlockSpec((rows, bw), lambda b, h, t: (b * per_seq + t, zb0 + h)),
                  pl.BlockSpec((rows, LANES), lambda b, h, t: (b * per_seq + t, 0)),
                  pl.BlockSpec((2 * heads, rows), lambda b, h, t: (0, b * per_seq + t)),
                  pl.BlockSpec((1, HEAD_DIM), lambda b, h, t: (0, 0)),
                  pl.BlockSpec((None, w_block, w_cols), lambda b, h, t: (layer, w_idx(b, h, t), 0))],
        out_specs=[qkv, pl.BlockSpec((w_block, w_cols), lambda b, h, t: (w_idx(b, h, t), 0))],
        out_shape=[jax.ShapeDtypeStruct((m, width), BF16), jax.ShapeDtypeStruct((w_rows, w_cols), BF16)],
        scratch_shapes=[pltpu.VMEM((hp, HEAD_DIM, HEAD_DIM), F32)],
        compiler_params=_params("arbitrary", "arbitrary", "arbitrary"),
        name="gated_delta",
    )(q, k, v, proj, gates, gates_t, dn_norm.reshape(1, HEAD_DIM), w_src)


def _merge_kernel(ga_ref, gb_ref, pa_ref, ob_ref, pp_ref, dp_ref, y_ref):
    ya = _sigmoid(ga_ref[...].astype(F32)) * _dot(pa_ref[...], pp_ref[...].astype(BF16))
    yb = _sigmoid(gb_ref[...].astype(F32)) * _dot(ob_ref[...], dp_ref[...].astype(BF16))
    y_ref[...] = (ya + yb).astype(y_ref.dtype)


def _merge(proj, gate_col0, pa, ob, pool_proj, dn_proj, layer):
    m, kp = pa.shape
    d = pool_proj.shape[2]
    tm, tn = min(MATMUL_ROWS, m), min(MATMUL_COLS, d)
    nb = d // tn
    assert gate_col0 % tn == 0
    g0 = gate_col0 // tn
    return pl.pallas_call(
        _merge_kernel,
        grid=(nb, m // tm),
        in_specs=[pl.BlockSpec((tm, tn), lambda j, i: (i, g0 + j)),
                  pl.BlockSpec((tm, tn), lambda j, i: (i, g0 + nb + j)),
                  pl.BlockSpec((tm, kp), lambda j, i: (i, 0)),
                  pl.BlockSpec((tm, kp), lambda j, i: (i, 0)),
                  pl.BlockSpec((None, kp, tn), lambda j, i: (layer, 0, j)),
                  pl.BlockSpec((None, kp, tn), lambda j, i: (layer, 0, j))],
        out_specs=pl.BlockSpec((tm, tn), lambda j, i: (i, j)),
        out_shape=jax.ShapeDtypeStruct((m, d), BF16),
        compiler_params=_params("parallel", "arbitrary"),
        name="gated_merge",
    )(proj, proj, pa, ob, pool_proj, dn_proj)


FFN_CARRY = 8


def _ffn_up_kernel(per_seq, ragged, h_ref, wa_ref, wu_ref, cw_ref, cb_ref, wsrc_ref, o_ref, w16_ref, a_ref):
    tm, tn = o_ref.shape
    half = tn // 2
    w16_ref[...] = wsrc_ref[...].astype(BF16)

    @pl.when(pl.program_id(1) % per_seq == 0)
    def _():
        a_ref[0:FFN_CARRY, :] = jnp.zeros((FFN_CARRY, tn), F32)

    def tile(width):
        cols = slice(0, width)
        h = h_ref[...]
        a_ref[FFN_CARRY:, cols] = _dot(h, wa_ref[:, cols])
        u = _dot(h, wu_ref[:, tn - width:])
        a = cb_ref[:, cols]
        for tap in range(FFN_CONV_TAPS):
            r0 = FFN_CARRY - (FFN_CONV_TAPS - 1) + tap
            a = a + a_ref[r0:r0 + tm, cols] * cw_ref[tap:tap + 1, cols]
        o_ref[:, cols] = (_silu(a) * u).astype(o_ref.dtype)
        a_ref[0:FFN_CARRY, cols] = a_ref[tm:tm + FFN_CARRY, cols]

    if ragged:
        last = pl.program_id(0) == pl.num_programs(0) - 1
        pl.when(jnp.logical_not(last))(lambda: tile(tn))
        pl.when(last)(lambda: tile(half))
    else:
        tile(tn)


def _ffn_up(h, w_up, layer, conv_w, conv_b, seq, w_src):
    m, d = h.shape
    n = w_up.shape[1] // 2
    tm, tn = min(MATMUL_ROWS, seq), MATMUL_COLS
    assert n % (tn // 2) == 0
    nt = pl.cdiv(n, tn)
    per_seq = seq // tm
    pad = nt * tn - n
    conv_w = jnp.pad(conv_w, ((0, 0), (0, 0), (0, pad)))
    conv_b = jnp.pad(conv_b, ((0, 0), (0, pad))).reshape(conv_b.shape[0], 1, nt * tn)
    row_tiles = m // tm
    _, w_rows, w_cols = w_src.shape
    w_block, w_blocks = _ride_along_blocks(w_rows, nt * row_tiles)
    w_idx = lambda j, i: jnp.minimum(j * row_tiles + i, w_blocks - 1)
    return pl.pallas_call(
        functools.partial(_ffn_up_kernel, per_seq, n % tn != 0),
        grid=(nt, row_tiles),
        in_specs=[pl.BlockSpec((tm, d), lambda j, i: (i, 0)),
                  pl.BlockSpec((d, tn), lambda j, i: (0, j)),
                  pl.BlockSpec((pl.Element(d), pl.Element(tn)),
                               lambda j, i: (0, pl.multiple_of(jnp.minimum(n + j * tn, 2 * n - tn), LANES))),
                  pl.BlockSpec((None, FFN_CONV_TAPS, tn), lambda j, i: (layer, 0, j)),
                  pl.BlockSpec((None, 1, tn), lambda j, i: (layer, 0, j)),
                  pl.BlockSpec((None, w_block, w_cols), lambda j, i: (layer, w_idx(j, i), 0))],
        out_specs=[pl.BlockSpec((tm, tn), lambda j, i: (i, j)),
                   pl.BlockSpec((w_block, w_cols), lambda j, i: (w_idx(j, i), 0))],
        out_shape=[jax.ShapeDtypeStruct((m, n), BF16), jax.ShapeDtypeStruct((w_rows, w_cols), BF16)],
        scratch_shapes=[pltpu.VMEM((tm + FFN_CARRY, tn), F32)],
        compiler_params=_params("arbitrary", "arbitrary"),
        name="ffn_up_conv_gate",
    )(h, w_up, w_up, conv_w, conv_b, w_src)


def kernel(x, c, w_ada, b_ada, ada_table, mix_pre_norm, w_in, pool_w, pool_scale, pool_proj,
           dn_conv, dn_a_log, dn_dt_bias, dn_norm, dn_proj, w_o, mix_post_norm,
           ffn_pre_norm, w_up, ffn_conv, ffn_conv_b, w_down, ffn_post_norm):
    batch, seq, d = x.shape
    depth = w_in.shape[0]
    m = batch * seq
    pool_width = pool_scale.shape[1]
    dn_width = dn_proj.shape[1]
    heads = dn_a_log.shape[1]
    c_qkv = pool_width
    c_z = c_qkv + 3 * dn_width
    c_ba = c_z + dn_width

    c8 = jnp.pad(c, ((0, SUBLANES - batch), (0, 0)))
    mod = _ada(c8, w_ada, b_ada, ada_table.reshape(depth, N_MOD * d))
    mod = mod[:, :batch].reshape(depth, batch, N_MOD, 1, d)

    xf = x.reshape(m, d)
    h = _norm_mod(xf, mix_pre_norm[0], mod[0, :, 1], mod[0, :, 0], seq)
    w_in_t = jnp.swapaxes(w_in, 1, 2)
    for l in range(depth):
        shift_f, scale_f, gate_m, gate_f = mod[l, :, 3], mod[l, :, 4], mod[l, :, 2], mod[l, :, 5]
        proj = _matmul_nt(h, w_in_t, l, c_ba + 2 * d, c_ba, 2 * heads, BF16, IN_PROJ_ROWS, MATMUL_COLS, "in_proj")
        gates = _ba(h, w_in_t, l, c_ba, dn_a_log[l], dn_dt_bias[l], heads)
        gates_t = gates[:, :2 * heads].T
        pa = _pool(proj, pool_w, l, pool_scale[l], seq)
        q, k, v = _qkv_conv(proj, dn_conv[l], dn_width, c_qkv, seq)
        ob, w_up16 = _delta(q, k, v, proj, c_z, gates, gates_t, dn_norm[l], batch, seq, heads, w_up, l)
        y = _merge(proj, c_ba, pa, ob, pool_proj, dn_proj, l)
        yo = _matmul(y, w_o, l, d, BF16, IN_PROJ_ROWS, MATMUL_COLS, "out_proj", rows_outer=True)
        xf, h = _resid_norm(yo, xf, mix_post_norm[l], gate_m, seq,
                            nxt=(ffn_pre_norm[l], scale_f, shift_f))
        act, w_down16 = _ffn_up(h, w_up16, l, ffn_conv, ffn_conv_b, seq, w_down)
        yd = _matmul(act, w_down16[None], 0, d, BF16, FFN_DOWN_ROWS, MATMUL_COLS, "ffn_down")
        if l + 1 < depth:
            xf, h = _resid_norm(yd, xf, ffn_post_norm[l], gate_f, seq,
                                nxt=(mix_pre_norm[l + 1], mod[l + 1, :, 1], mod[l + 1, :, 0]))
        else:
            xf = _resid_norm(yd, xf, ffn_post_norm[l], gate_f, seq)
    return xf.reshape(batch, seq, d)
```
